```python
import jax, jax.numpy as jnp
from jax import lax
import numpy as np

D_MODEL = 1024
BATCH = 4
SEQ = 8192
DEPTH = 4

N_A = DEPTH // 2
N_B = DEPTH - N_A
N_HEADS = 16
HEAD_DIM = D_MODEL // N_HEADS
CONV_W = 3
D_FF = 2816
Q_BLOCK = 128
EPS = 1e-6

kernel_name = "yoco_shortconv_fox_hybrid"


def rmsnorm(x, g):
    xf = x.astype(jnp.float32)
    y = xf * lax.rsqrt(jnp.mean(xf * xf, axis=-1, keepdims=True) + EPS)
    return (y * g).astype(x.dtype)


def causal_dwconv(u, w):
    width = w.shape[0]
    s = u.shape[1]
    up = jnp.pad(u, ((0, 0), (width - 1, 0), (0, 0)))
    return sum(up[:, i:i + s] * w[i] for i in range(width))


def short_conv_mixer(xn, w_in, conv_w, w_out):
    proj = xn @ w_in
    b, c, h = jnp.split(proj, 3, axis=-1)
    u = causal_dwconv(c * h, conv_w)
    return (b * u) @ w_out


def conv_ffn(xn, w_up, conv_w, w_down):
    up = xn @ w_up
    a, g = jnp.split(up, 2, axis=-1)
    a = causal_dwconv(a, conv_w)
    return (jax.nn.silu(a) * g) @ w_down


def forgetting_attention(q, k, v, c):
    bsz, nh, s_len, hd = q.shape
    nb = s_len // Q_BLOCK
    scale = hd ** -0.5
    qb = q.reshape(bsz, nh, nb, Q_BLOCK, hd).transpose(2, 0, 1, 3, 4)
    cb = c.reshape(bsz, nh, nb, Q_BLOCK).transpose(2, 0, 1, 3)
    kpos = jnp.arange(s_len)

    def one_block(args):
        q_i, c_i, i = args
        s = jnp.einsum('bhqd,bhkd->bhqk', q_i, k, preferred_element_type=jnp.float32) * scale
        s = s + c_i[..., None] - c[:, :, None, :]
        qpos = i * Q_BLOCK + jnp.arange(Q_BLOCK)
        s = jnp.where(kpos[None, :] <= qpos[:, None], s, -jnp.inf)
        p = jax.nn.softmax(s, axis=-1)
        return jnp.einsum('bhqk,bhkd->bhqd', p.astype(v.dtype), v)

    o = lax.map(one_block, (qb, cb, jnp.arange(nb)))
    return o.transpose(1, 0, 3, 2, 4).reshape(bsz, s_len, nh * hd)


def setup_inputs(seed: int = 0) -> dict:
    key = jax.random.key(seed)
    ks = jax.random.split(key, 17)
    f32 = jnp.float32
    out_scale = (2 * DEPTH) ** -0.5

    def nrm(k, shape, scale):
        return jax.random.normal(k, shape, f32) * scale

    def gain(k, shape):
        return 1.0 + 0.02 * jax.random.normal(k, shape, f32)

    x = nrm(ks[0], (BATCH, SEQ, D_MODEL), 1.0)
    attn_norm = gain(ks[1], (DEPTH, D_MODEL))
    ffn_norm = gain(ks[2], (DEPTH, D_MODEL))
    a_w_in = nrm(ks[3], (N_A, D_MODEL, 3 * D_MODEL), D_MODEL ** -0.5)
    a_conv = nrm(ks[4], (N_A, CONV_W, D_MODEL), CONV_W ** -0.5)
    a_w_out = nrm(ks[5], (N_A, D_MODEL, D_MODEL), out_scale * D_MODEL ** -0.5)
    kv_norm = gain(ks[6], (D_MODEL,))
    w_kvf = jnp.concatenate([
        nrm(ks[7], (D_MODEL, 2 * D_MODEL), D_MODEL ** -0.5),
        nrm(ks[8], (D_MODEL, N_HEADS), 0.1 * D_MODEL ** -0.5),
    ], axis=1)
    b_f = jax.random.uniform(ks[9], (N_HEADS,), f32, 1.0, 6.0)
    k_norm = gain(ks[10], (HEAD_DIM,))
    b_w_qg = nrm(ks[11], (N_B, D_MODEL, 2 * D_MODEL), D_MODEL ** -0.5)
    q_norm = gain(ks[12], (N_B, HEAD_DIM))
    b_w_out = nrm(ks[13], (N_B, D_MODEL, D_MODEL), out_scale * D_MODEL ** -0.5)
    ffn_w_up = nrm(ks[14], (DEPTH, D_MODEL, 2 * D_FF), D_MODEL ** -0.5)
    ffn_conv = nrm(ks[15], (DEPTH, CONV_W, D_FF), CONV_W ** -0.5)
    ffn_w_down = nrm(ks[16], (DEPTH, D_FF, D_MODEL), out_scale * D_FF ** -0.5)
    return {"x": x, "attn_norm": attn_norm, "ffn_norm": ffn_norm,
            "a_w_in": a_w_in, "a_conv": a_conv, "a_w_out": a_w_out,
            "kv_norm": kv_norm, "w_kvf": w_kvf, "b_f": b_f, "k_norm": k_norm,
            "b_w_qg": b_w_qg, "q_norm": q_norm, "b_w_out": b_w_out,
            "ffn_w_up": ffn_w_up, "ffn_conv": ffn_conv, "ffn_w_down": ffn_w_down}


def reference(x, attn_norm, ffn_norm, a_w_in, a_conv, a_w_out, kv_norm, w_kvf, b_f,
              k_norm, b_w_qg, q_norm, b_w_out, ffn_w_up, ffn_conv, ffn_w_down):
    bsz, s_len, d = x.shape
    k = v = c = None
    for l in range(DEPTH):
        if l < N_A:
            xn = rmsnorm(x, attn_norm[l])
            x = x + short_conv_mixer(xn, a_w_in[l], a_conv[l], a_w_out[l])
        else:
            if l == N_A:
                h = rmsnorm(x, kv_norm)
                kvf = h @ w_kvf
                k_s = kvf[..., :d].reshape(bsz, s_len, N_HEADS, HEAD_DIM)
                v_s = kvf[..., d:2 * d].reshape(bsz, s_len, N_HEADS, HEAD_DIM)
                f_logit = (kvf[..., 2 * d:] + b_f).astype(jnp.float32)
                k = rmsnorm(k_s, k_norm).transpose(0, 2, 1, 3)
                v = v_s.transpose(0, 2, 1, 3)
                c = jnp.cumsum(jax.nn.log_sigmoid(f_logit), axis=1).transpose(0, 2, 1)
            j = l - N_A
            xn = rmsnorm(x, attn_norm[l])
            qg = xn @ b_w_qg[j]
            q = rmsnorm(qg[..., :d].reshape(bsz, s_len, N_HEADS, HEAD_DIM), q_norm[j])
            q = q.transpose(0, 2, 1, 3)
            o = forgetting_attention(q, k, v, c)
            o = o * jax.nn.sigmoid(qg[..., d:])
            x = x + o @ b_w_out[j]
        xn = rmsnorm(x, ffn_norm[l])
        x = x + conv_ffn(xn, ffn_w_up[l], ffn_conv[l], ffn_w_down[l])
    return x
```

```python
import functools

import jax
import jax.numpy as jnp
from jax import lax
from jax.experimental import pallas as pl
from jax.experimental.pallas import tpu as pltpu

F32 = jnp.float32
BF16 = jnp.bfloat16

EPS = 1e-6
N_HEADS = 16
HEAD_DIM = 64
LANES = 128
PAIR = 2 * HEAD_DIM
CARRY_ROWS = 8

ROW_TILE = 512
CHUNK = 256
ATTN_TQ = 512
ATTN_TK = 512
VMEM_LIMIT = 56 * 1024 * 1024


def _dot(a, b):
    return jnp.dot(a, b, preferred_element_type=F32)


def _rms(x, g):
    return x * lax.rsqrt(jnp.mean(x * x, axis=-1, keepdims=True) + EPS) * g


def _split2(x):
    hi = x.astype(BF16)
    lo = (x - hi.astype(F32)).astype(BF16)
    return hi, lo


def _split3(x):
    p1 = x.astype(BF16)
    r1 = x - p1.astype(F32)
    p2 = r1.astype(BF16)
    p3 = (r1 - p2.astype(F32)).astype(BF16)
    return p1, p2, p3


def _head_rms(q, red_ref, bcast_ref):
    sq_hi, sq_lo = _split2(q * q)
    ms = _dot(jnp.concatenate([sq_hi, sq_lo], axis=1), red_ref[...])
    r_hi, r_lo = _split2(lax.rsqrt(ms + EPS))
    return q * _dot(jnp.concatenate([r_hi, r_lo], axis=1), bcast_ref[...])


def _shift_down(a, first_row):
    row = lax.broadcasted_iota(jnp.int32, a.shape, 0)
    return jnp.where(row == 0, first_row, pltpu.roll(a, 1, axis=0))


def _causal_conv3(u, w, carry_ref, lo, hi):
    rows = u.shape[0]
    old = carry_ref[:, lo:hi]
    carry_ref[:, lo:hi] = u[rows - CARRY_ROWS:, :]
    u1 = _shift_down(u, old[CARRY_ROWS - 1:CARRY_ROWS, :])
    u2 = _shift_down(u1, old[CARRY_ROWS - 2:CARRY_ROWS - 1, :])
    return w[2:3, :] * u + w[1:2, :] * u1 + w[0:1, :] * u2


def _mixer(x, gn_ref, w_in_ref, cw_ref, w_out_ref, carry_ref):
    d = x.shape[1]
    xn = _rms(x, gn_ref[...]).astype(BF16)
    acc = jnp.zeros(x.shape, F32)
    for j in range(d // CHUNK):
        lo, hi = j * CHUNK, (j + 1) * CHUNK
        b = _dot(xn, w_in_ref[:, lo:hi])
        c = _dot(xn, w_in_ref[:, d + lo:d + hi])
        h = _dot(xn, w_in_ref[:, 2 * d + lo:2 * d + hi])
        u = _causal_conv3(c * h, cw_ref[:, lo:hi], carry_ref, lo, hi)
        acc = acc + _dot((b * u).astype(BF16), w_out_ref[lo:hi, :])
    return x + acc


def _conv_ffn(x, gn_ref, w_up_ref, cw_ref, w_down_ref, carry_ref):
    f = w_down_ref.shape[0]
    xn = _rms(x, gn_ref[...]).astype(BF16)
    acc = jnp.zeros(x.shape, F32)
    for j in range(f // CHUNK):
        lo, hi = j * CHUNK, (j + 1) * CHUNK
        a = _dot(xn, w_up_ref[:, lo:hi])
        g = _dot(xn, w_up_ref[:, f + lo:f + hi])
        a = _causal_conv3(a, cw_ref[:, lo:hi], carry_ref, lo, hi)
        hid = a * jax.nn.sigmoid(a) * g
        acc = acc + _dot(hid.astype(BF16), w_down_ref[lo:hi, :])
    return x + acc


def _reset_at_sequence_start(*carry_refs):
    @pl.when(pl.program_id(1) == 0)
    def _():
        for ref in carry_refs:
            ref[...] = jnp.zeros(ref.shape, ref.dtype)


def _layer_a_kernel(x_ref, an_ref, w_in_ref, acw_ref, w_out_ref,
                    fn_ref, w_up_ref, fcw_ref, w_down_ref, o_ref,
                    mix_carry, ffn_carry):
    _reset_at_sequence_start(mix_carry, ffn_carry)
    x = _mixer(x_ref[...], an_ref, w_in_ref, acw_ref, w_out_ref, mix_carry)
    o_ref[...] = _conv_ffn(x, fn_ref, w_up_ref, fcw_ref, w_down_ref, ffn_carry)


def _layer_b_kernel(x_ref, og_ref, w_out_ref,
                    fn_ref, w_up_ref, fcw_ref, w_down_ref, o_ref, ffn_carry):
    _reset_at_sequence_start(ffn_carry)
    x = x_ref[...] + _dot(og_ref[...], w_out_ref[...])
    o_ref[...] = _conv_ffn(x, fn_ref, w_up_ref, fcw_ref, w_down_ref, ffn_carry)


def _interleave_pairs(main, extra, out_ref):
    for p in range(main.shape[1] // PAIR):
        out_ref[:, 2 * p * PAIR:(2 * p + 1) * PAIR] = main[:, p * PAIR:(p + 1) * PAIR].astype(out_ref.dtype)
        out_ref[:, (2 * p + 1) * PAIR:(2 * p + 2) * PAIR] = extra[:, p * PAIR:(p + 1) * PAIR].astype(out_ref.dtype)


def _kv_kernel(x_ref, gn_ref, wk_ref, wv_ref, wf_ref, bf_ref, kg_ref,
               red_ref, bcast_ref, tri_ref, place_ref, const_ref,
               kcat_ref, v_ref, cp_ref, c_carry):
    _reset_at_sequence_start(c_carry)
    h = _rms(x_ref[...], gn_ref[...]).astype(BF16)
    v_ref[...] = _dot(h, wv_ref[...]).astype(v_ref.dtype)

    f_logit = _dot(h, wf_ref[...]) + bf_ref[...]
    log_f = jnp.minimum(f_logit, 0.0) - jnp.log1p(jnp.exp(-jnp.abs(f_logit)))
    tri = tri_ref[...]
    c = c_carry[0:1, :] + sum(_dot(tri, piece) for piece in _split3(log_f))
    c_carry[0:1, :] = c[c.shape[0] - 1:, :]
    c1, c2, c3 = _split3(c)
    lane = lax.broadcasted_iota(jnp.int32, c.shape, 1)
    cp = jnp.where(lane < N_HEADS, c1, jnp.where(lane < 2 * N_HEADS, c2, c3))
    cp_ref[...] = cp

    k = _head_rms(_dot(h, wk_ref[...]), red_ref, bcast_ref) * kg_ref[...]
    extra = _dot(cp, place_ref[...]) + const_ref[...]
    _interleave_pairs(k, extra, kcat_ref)


def _q_kernel(x_ref, gn_ref, wq_ref, wg_ref, qg_ref, cp_ref,
              red_ref, bcast_ref, place_ref, const_ref,
              qcat_ref, gate_ref):
    xn = _rms(x_ref[...], gn_ref[...]).astype(BF16)
    gate_ref[...] = jax.nn.sigmoid(_dot(xn, wg_ref[...]))
    q = _head_rms(_dot(xn, wq_ref[...]), red_ref, bcast_ref) * qg_ref[...]
    extra = _dot(cp_ref[...], place_ref[...]) + const_ref[...]
    _interleave_pairs(q, extra, qcat_ref)


def _attn_kernel(q_ref, k_ref, v_ref, gate_ref, o_ref):
    tq = q_ref.shape[0]
    qi = pl.program_id(2)
    q = q_ref[...]
    lane_q = lax.broadcasted_iota(jnp.int32, (1, q.shape[1]), 1)
    row = lax.broadcasted_iota(jnp.int32, (tq, ATTN_TK), 0)
    col = lax.broadcasted_iota(jnp.int32, (tq, ATTN_TK), 1)

    def attend(qh, j, carry, on_diagonal):
        m, l, acc = carry
        start = pl.multiple_of(j * ATTN_TK, ATTN_TK)
        kb = k_ref[pl.ds(start, ATTN_TK), :]
        vb = v_ref[pl.ds(start, ATTN_TK), :]
        s = lax.dot_general(qh, kb, (((1,), (1,)), ((), ())), preferred_element_type=F32)
        if on_diagonal:
            s = jnp.where(col <= row, s, -jnp.inf)
        m_new = jnp.maximum(m, jnp.max(s, axis=1, keepdims=True))
        alpha = jnp.exp(m - m_new)
        p = jnp.exp(s - m_new)
        l = alpha * l + jnp.sum(p, axis=1, keepdims=True)
        acc = alpha * acc + _dot(p.astype(BF16), vb)
        return m_new, l, acc

    outs = []
    for head in range(2):
        qh = jnp.where((lane_q % PAIR) // HEAD_DIM == head, q, jnp.zeros_like(q))
        init = (jnp.full((tq, 1), -jnp.inf, F32), jnp.zeros((tq, 1), F32), jnp.zeros((tq, PAIR), F32))
        carry = lax.fori_loop(0, qi, lambda j, c: attend(qh, j, c, False), init)
        _, l, acc = attend(qh, qi, carry, True)
        outs.append(acc / l)
    lane_o = lax.broadcasted_iota(jnp.int32, (1, PAIR), 1)
    o = jnp.where(lane_o < HEAD_DIM, outs[0], outs[1])
    o_ref[...] = (o * gate_ref[...]).astype(o_ref.dtype)


def _resident(shape):
    return pl.BlockSpec(shape, lambda *_: (0,) * len(shape), pipeline_mode=pl.Buffered(1))


def _row_tile(cols):
    return pl.BlockSpec((None, ROW_TILE, cols), lambda b, s: (b, s, 0))


def _dense_call(kernel, name, grid, in_specs, out_specs, out_shape, scratch_shapes=()):
    return pl.pallas_call(
        kernel, name=name, grid=grid, in_specs=in_specs, out_specs=out_specs, out_shape=out_shape,
        scratch_shapes=list(scratch_shapes),
        compiler_params=pltpu.CompilerParams(
            dimension_semantics=("arbitrary", "arbitrary"), vmem_limit_bytes=VMEM_LIMIT))


def _placement_constants():
    d = N_HEADS * HEAD_DIM
    head_of_col = jnp.arange(d) // HEAD_DIM
    pos_in_head = jnp.arange(d) % HEAD_DIM
    lane = jnp.arange(LANES)
    red = (head_of_col[:, None] == lane[None, :]).astype(F32) / HEAD_DIM
    red = jnp.concatenate([red, red], axis=0).astype(BF16)
    bcast = (lane[:, None] == head_of_col[None, :]).astype(BF16)
    bcast = jnp.concatenate([bcast, bcast], axis=0)
    piece, head = lane // N_HEADS, lane % N_HEADS
    valid = (piece < 3)[:, None] & (head[:, None] == head_of_col[None, :])
    place_q = (valid & (pos_in_head[None, :] == piece[:, None])).astype(BF16)
    place_k = -(valid & (pos_in_head[None, :] == 3 + piece[:, None])).astype(BF16)
    const_q = ((pos_in_head >= 3) & (pos_in_head < 6)).astype(F32)[None, :]
    const_k = (pos_in_head < 3).astype(F32)[None, :]
    return red, bcast, place_q, const_q, place_k, const_k


def kernel(x, attn_norm, ffn_norm, a_w_in, a_conv, a_w_out, kv_norm, w_kvf, b_f, k_norm,
           b_w_qg, q_norm, b_w_out, ffn_w_up, ffn_conv, ffn_w_down):
    bsz, s_len, d = x.shape
    depth = ffn_w_up.shape[0]
    n_a = a_w_in.shape[0]
    f = ffn_w_down.shape[1]
    assert d == N_HEADS * HEAD_DIM and s_len % ROW_TILE == 0 and s_len % ATTN_TQ == 0
    assert ATTN_TQ == ATTN_TK and d % CHUNK == 0 and f % CHUNK == 0

    grid = (bsz, s_len // ROW_TILE)
    x_shape = jax.ShapeDtypeStruct((bsz, s_len, d), F32)
    red, bcast, place_q, const_q, place_k, const_k = _placement_constants()
    row = lambda v: v.reshape(1, -1)

    def ffn_operands(l):
        return (row(ffn_norm[l]), ffn_w_up[l].astype(BF16), ffn_conv[l], ffn_w_down[l].astype(BF16))

    ffn_specs = [_resident((1, d)), _resident((d, 2 * f)), _resident((3, f)), _resident((f, d))]
    ffn_carry = pltpu.VMEM((CARRY_ROWS, f), F32)

    layer_a = _dense_call(
        _layer_a_kernel, "layer_a", grid,
        [_row_tile(d), _resident((1, d)), _resident((d, 3 * d)), _resident((3, d)), _resident((d, d))] + ffn_specs,
        _row_tile(d), x_shape, [pltpu.VMEM((CARRY_ROWS, d), F32), ffn_carry])
    for l in range(n_a):
        x = layer_a(x, row(attn_norm[l]), a_w_in[l].astype(BF16), a_conv[l], a_w_out[l].astype(BF16),
                    *ffn_operands(l))

    w_f = w_kvf[:, 2 * d:]
    pad = LANES - 3 * N_HEADS
    w_f3 = jnp.pad(jnp.concatenate([w_f, w_f, w_f], axis=1), ((0, 0), (0, pad))).astype(BF16)
    b_f3 = jnp.pad(jnp.concatenate([b_f, b_f, b_f]), (0, pad)).reshape(1, LANES)
    tri = (jnp.arange(ROW_TILE)[:, None] >= jnp.arange(ROW_TILE)[None, :]).astype(BF16)
    kcat, v, cp = _dense_call(
        _kv_kernel, "kv_proj", grid,
        [_row_tile(d), _resident((1, d)), _resident((d, d)), _resident((d, d)), _resident((d, LANES)),
         _resident((1, LANES)), _resident((1, d)), _resident((2 * d, LANES)), _resident((2 * LANES, d)),
         _resident((ROW_TILE, ROW_TILE)), _resident((LANES, d)), _resident((1, d))],
        [_row_tile(2 * d), _row_tile(d), _row_tile(LANES)],
        [jax.ShapeDtypeStruct((bsz, s_len, 2 * d), BF16), jax.ShapeDtypeStruct((bsz, s_len, d), BF16),
         jax.ShapeDtypeStruct((bsz, s_len, LANES), BF16)],
        [pltpu.VMEM((CARRY_ROWS, LANES), F32)],
    )(x, row(kv_norm), w_kvf[:, :d].astype(BF16), w_kvf[:, d:2 * d].astype(BF16), w_f3, b_f3,
      row(jnp.tile(k_norm, N_HEADS)), red, bcast, tri, place_k, const_k)

    q_proj = _dense_call(
        _q_kernel, "q_proj", grid,
        [_row_tile(d), _resident((1, d)), _resident((d, d)), _resident((d, d)), _resident((1, d)),
         _row_tile(LANES), _resident((2 * d, LANES)), _resident((2 * LANES, d)),
         _resident((LANES, d)), _resident((1, d))],
        [_row_tile(2 * d), _row_tile(d)],
        [jax.ShapeDtypeStruct((bsz, s_len, 2 * d), BF16), x_shape])

    n_pairs = d // PAIR
    attention = pl.pallas_call(
        _attn_kernel, name="fox_attention", grid=(bsz, n_pairs, s_len // ATTN_TQ),
        in_specs=[pl.BlockSpec((None, ATTN_TQ, 2 * PAIR), lambda b, p, i: (b, i, p)),
                  pl.BlockSpec((None, s_len, 2 * PAIR), lambda b, p, i: (b, 0, p)),
                  pl.BlockSpec((None, s_len, PAIR), lambda b, p, i: (b, 0, p)),
                  pl.BlockSpec((None, ATTN_TQ, PAIR), lambda b, p, i: (b, i, p))],
        out_specs=pl.BlockSpec((None, ATTN_TQ, PAIR), lambda b, p, i: (b, i, p)),
        out_shape=jax.ShapeDtypeStruct((bsz, s_len, d), BF16),
        compiler_params=pltpu.CompilerParams(
            dimension_semantics=("arbitrary", "arbitrary", "arbitrary"), vmem_limit_bytes=VMEM_LIMIT))

    layer_b = _dense_call(
        _layer_b_kernel, "layer_b", grid,
        [_row_tile(d), _row_tile(d), _resident((d, d))] + ffn_specs,
        _row_tile(d), x_shape, [ffn_carry])

    scale = HEAD_DIM ** -0.5
    for l in range(n_a, depth):
        j = l - n_a
        qcat, gate = q_proj(x, row(attn_norm[l]), b_w_qg[j][:, :d].astype(BF16), b_w_qg[j][:, d:].astype(BF16),
                            row(jnp.tile(q_norm[j], N_HEADS)) * scale, cp, red, bcast, place_q, const_q)
        og = attention(qcat, kcat, v, gate)
        x = layer_b(x, og, b_w_out[j].astype(BF16), *ffn_operands(l))
    return x
```

```python
import functools

import jax
import jax.numpy as jnp
from jax import lax
from jax.experimental import pallas as pl
from jax.experimental.pallas import tpu as pltpu

F32 = jnp.float32
BF16 = jnp.bfloat16

EPS = 1e-6
LOG2_E = 1.4426950408889634
N_HEADS = 16
HEAD_DIM = 64
LANES = 128
PAIR = 2 * HEAD_DIM
CARRY_ROWS = 8

ROW_TILE = 512
CHUNK = 256
ATTN_TQ = 512
ATTN_TK = 512
VMEM_LIMIT = 56 * 1024 * 1024


def _dot(a, b):
    return jnp.dot(a, b, preferred_element_type=F32)


def _rms(x, g):
    return x * lax.rsqrt(jnp.mean(x * x, axis=-1, keepdims=True) + EPS) * g


def _split2(x):
    hi = x.astype(BF16)
    lo = (x - hi.astype(F32)).astype(BF16)
    return hi, lo


def _split3(x):
    p1 = x.astype(BF16)
    r1 = x - p1.astype(F32)
    p2 = r1.astype(BF16)
    p3 = (r1 - p2.astype(F32)).astype(BF16)
    return p1, p2, p3


def _head_rms(q, red_ref, bcast_ref):
    sq_hi, sq_lo = _split2(q * q)
    ms = _dot(jnp.concatenate([sq_hi, sq_lo], axis=1), red_ref[...])
    r_hi, r_lo = _split2(lax.rsqrt(ms + EPS))
    return q * _dot(jnp.concatenate([r_hi, r_lo], axis=1), bcast_ref[...])


def _shift_down(a, first_row):
    row = lax.broadcasted_iota(jnp.int32, a.shape, 0)
    return jnp.where(row == 0, first_row, pltpu.roll(a, 1, axis=0))


def _causal_conv3(u, w, carry_ref, lo, hi):
    rows = u.shape[0]
    old = carry_ref[:, lo:hi]
    carry_ref[:, lo:hi] = u[rows - CARRY_ROWS:, :]
    u1 = _shift_down(u, old[CARRY_ROWS - 1:CARRY_ROWS, :])
    u2 = _shift_down(u1, old[CARRY_ROWS - 2:CARRY_ROWS - 1, :])
    return w[2:3, :] * u + w[1:2, :] * u1 + w[0:1, :] * u2


def _mixer(x, gn_ref, w_in_ref, cw_ref, w_out_ref, carry_ref):
    d = x.shape[1]
    xn = _rms(x, gn_ref[...]).astype(BF16)
    acc = jnp.zeros(x.shape, F32)
    for j in range(d // CHUNK):
        lo, hi = j * CHUNK, (j + 1) * CHUNK
        b = _dot(xn, w_in_ref[:, lo:hi])
        c = _dot(xn, w_in_ref[:, d + lo:d + hi])
        h = _dot(xn, w_in_ref[:, 2 * d + lo:2 * d + hi])
        u = _causal_conv3(c * h, cw_ref[:, lo:hi], carry_ref, lo, hi)
        acc = acc + _dot((b * u).astype(BF16), w_out_ref[lo:hi, :])
    return x + acc


def _conv_ffn(x, gn_ref, w_up_ref, cw_ref, w_down_ref, carry_ref):
    f = w_down_ref.shape[0]
    xn = _rms(x, gn_ref[...]).astype(BF16)
    acc = jnp.zeros(x.shape, F32)
    for j in range(f // CHUNK):
        lo, hi = j * CHUNK, (j + 1) * CHUNK
        a = _dot(xn, w_up_ref[:, lo:hi])
        g = _dot(xn, w_up_ref[:, f + lo:f + hi])
        a = _causal_conv3(a, cw_ref[:, lo:hi], carry_ref, lo, hi)
        hid = a * jax.nn.sigmoid(a) * g
        acc = acc + _dot(hid.astype(BF16), w_down_ref[lo:hi, :])
    return x + acc


def _reset_at_sequence_start(*carry_refs):
    @pl.when(pl.program_id(1) == 0)
    def _():
        for ref in carry_refs:
            ref[...] = jnp.zeros(ref.shape, ref.dtype)


def _layer_a_kernel(x_ref, an_ref, w_in_ref, acw_ref, w_out_ref,
                    fn_ref, w_up_ref, fcw_ref, w_down_ref, o_ref,
                    mix_carry, ffn_carry):
    _reset_at_sequence_start(mix_carry, ffn_carry)
    x = _mixer(x_ref[...], an_ref, w_in_ref, acw_ref, w_out_ref, mix_carry)
    o_ref[...] = _conv_ffn(x, fn_ref, w_up_ref, fcw_ref, w_down_ref, ffn_carry)


def _layer_b_kernel(x_ref, og_ref, w_out_ref,
                    fn_ref, w_up_ref, fcw_ref, w_down_ref, o_ref, ffn_carry):
    _reset_at_sequence_start(ffn_carry)
    x = x_ref[...] + _dot(og_ref[...], w_out_ref[...])
    o_ref[...] = _conv_ffn(x, fn_ref, w_up_ref, fcw_ref, w_down_ref, ffn_carry)


def _interleave_pairs(main, extra, out_ref):
    for p in range(main.shape[1] // PAIR):
        out_ref[:, 2 * p * PAIR:(2 * p + 1) * PAIR] = main[:, p * PAIR:(p + 1) * PAIR].astype(out_ref.dtype)
        out_ref[:, (2 * p + 1) * PAIR:(2 * p + 2) * PAIR] = extra[:, p * PAIR:(p + 1) * PAIR].astype(out_ref.dtype)


def _kv_kernel(x_ref, gn_ref, wk_ref, wv_ref, wf_ref, bf_ref, kg_ref,
               red_ref, bcast_ref, tri_ref, place_ref, const_ref,
               kcat_ref, vt_ref, cp_ref, c_carry):
    _reset_at_sequence_start(c_carry)
    h = _rms(x_ref[...], gn_ref[...]).astype(BF16)
    vt_ref[...] = _dot(h, wv_ref[...]).T.astype(vt_ref.dtype)

    f_logit = _dot(h, wf_ref[...]) + bf_ref[...]
    log_f = jnp.minimum(f_logit, 0.0) - jnp.log1p(jnp.exp(-jnp.abs(f_logit)))
    tri = tri_ref[...]
    c = c_carry[0:1, :] + sum(_dot(tri, piece) for piece in _split3(log_f))
    c_carry[0:1, :] = c[c.shape[0] - 1:, :]
    c1, c2, c3 = _split3(c * LOG2_E)
    lane = lax.broadcasted_iota(jnp.int32, c.shape, 1)
    cp = jnp.where(lane < N_HEADS, c1, jnp.where(lane < 2 * N_HEADS, c2, c3))
    cp_ref[...] = cp

    k = _head_rms(_dot(h, wk_ref[...]), red_ref, bcast_ref) * kg_ref[...]
    extra = _dot(cp, place_ref[...]) + const_ref[...]
    _interleave_pairs(k, extra, kcat_ref)


def _q_kernel(x_ref, gn_ref, wq_ref, wg_ref, qg_ref, cp_ref,
              red_ref, bcast_ref, place_ref, const_ref,
              qcat_ref, gate_ref):
    xn = _rms(x_ref[...], gn_ref[...]).astype(BF16)
    gate_ref[...] = jax.nn.sigmoid(_dot(xn, wg_ref[...]))
    q = _head_rms(_dot(xn, wq_ref[...]), red_ref, bcast_ref) * qg_ref[...]
    extra = _dot(cp_ref[...], place_ref[...]) + const_ref[...]
    _interleave_pairs(q, extra, qcat_ref)


def _attn_kernel(q_ref, k_ref, vt_ref, gate_ref, o_ref, s_scr, m_scr, acc_scr):
    tq = q_ref.shape[0]
    qi = pl.program_id(2)
    q = q_ref[...]
    lane_q = lax.broadcasted_iota(jnp.int32, (1, q.shape[1]), 1)
    q_heads = [jnp.where((lane_q % PAIR) // HEAD_DIM == head, q, jnp.zeros_like(q)) for head in range(2)]
    ones = jnp.ones((HEAD_DIM, ATTN_TK), BF16)

    def produce(j, slot):
        start = pl.multiple_of(j * ATTN_TK, ATTN_TK)
        kb = k_ref[pl.ds(start, ATTN_TK), :]
        for head in range(2):
            s_scr[slot, head] = lax.dot_general(kb, q_heads[head], (((1,), (1,)), ((), ())),
                                                preferred_element_type=F32)

    def consume(j, slot, on_diagonal):
        for head in range(2):
            s = s_scr[slot, head]
            if on_diagonal:
                key_pos = lax.broadcasted_iota(jnp.int32, s.shape, 0)
                query_pos = lax.broadcasted_iota(jnp.int32, s.shape, 1)
                s = jnp.where(key_pos <= query_pos, s, -jnp.inf)
            m = m_scr[head]
            m_new = jnp.maximum(m, jnp.max(s, axis=0, keepdims=True))
            p = jnp.exp2(s - m_new).astype(BF16)
            v_aug = jnp.concatenate([vt_ref[j, head * HEAD_DIM:(head + 1) * HEAD_DIM, :], ones], axis=0)
            acc_scr[head] = jnp.exp2(m - m_new) * acc_scr[head] + _dot(v_aug, p)
            m_scr[head] = m_new

    m_scr[...] = jnp.full(m_scr.shape, -jnp.inf, F32)
    acc_scr[...] = jnp.zeros(acc_scr.shape, F32)
    produce(0, 0)

    @pl.loop(0, qi // 2)
    def _(i):
        produce(2 * i + 1, 1)
        consume(2 * i, 0, False)
        produce(2 * i + 2, 0)
        consume(2 * i + 1, 1, False)

    @pl.when(qi % 2 == 0)
    def _():
        consume(qi, 0, True)

    @pl.when(qi % 2 == 1)
    def _():
        produce(qi, 1)
        consume(qi - 1, 0, False)
        consume(qi, 1, True)

    o_t = jnp.concatenate([acc_scr[head, :HEAD_DIM, :] / acc_scr[head, HEAD_DIM:, :] for head in range(2)], axis=0)
    o_ref[...] = (o_t.T * gate_ref[...]).astype(o_ref.dtype)


def _resident(shape):
    return pl.BlockSpec(shape, lambda *_: (0,) * len(shape), pipeline_mode=pl.Buffered(1))


def _row_tile(cols):
    return pl.BlockSpec((None, ROW_TILE, cols), lambda b, s: (b, s, 0))


def _dense_call(kernel, name, grid, in_specs, out_specs, out_shape, scratch_shapes=()):
    return pl.pallas_call(
        kernel, name=name, grid=grid, in_specs=in_specs, out_specs=out_specs, out_shape=out_shape,
        scratch_shapes=list(scratch_shapes),
        compiler_params=pltpu.CompilerParams(
            dimension_semantics=("arbitrary", "arbitrary"), vmem_limit_bytes=VMEM_LIMIT))


def _placement_constants():
    d = N_HEADS * HEAD_DIM
    head_of_col = jnp.arange(d) // HEAD_DIM
    pos_in_head = jnp.arange(d) % HEAD_DIM
    lane = jnp.arange(LANES)
    red = (head_of_col[:, None] == lane[None, :]).astype(F32) / HEAD_DIM
    red = jnp.concatenate([red, red], axis=0).astype(BF16)
    bcast = (lane[:, None] == head_of_col[None, :]).astype(BF16)
    bcast = jnp.concatenate([bcast, bcast], axis=0)
    piece, head = lane // N_HEADS, lane % N_HEADS
    valid = (piece < 3)[:, None] & (head[:, None] == head_of_col[None, :])
    place_q = (valid & (pos_in_head[None, :] == piece[:, None])).astype(BF16)
    place_k = -(valid & (pos_in_head[None, :] == 3 + piece[:, None])).astype(BF16)
    const_q = ((pos_in_head >= 3) & (pos_in_head < 6)).astype(F32)[None, :]
    const_k = (pos_in_head < 3).astype(F32)[None, :]
    return red, bcast, place_q, const_q, place_k, const_k


def kernel(x, attn_norm, ffn_norm, a_w_in, a_conv, a_w_out, kv_norm, w_kvf, b_f, k_norm,
           b_w_qg, q_norm, b_w_out, ffn_w_up, ffn_conv, ffn_w_down):
    bsz, s_len, d = x.shape
    depth = ffn_w_up.shape[0]
    n_a = a_w_in.shape[0]
    f = ffn_w_down.shape[1]
    assert d == N_HEADS * HEAD_DIM and s_len % ROW_TILE == 0 and s_len % ATTN_TQ == 0
    assert ATTN_TQ == ATTN_TK == ROW_TILE and d % CHUNK == 0 and f % CHUNK == 0

    grid = (bsz, s_len // ROW_TILE)
    x_shape = jax.ShapeDtypeStruct((bsz, s_len, d), F32)
    red, bcast, place_q, const_q, place_k, const_k = _placement_constants()
    row = lambda v: v.reshape(1, -1)

    def ffn_operands(l):
        return (row(ffn_norm[l]), ffn_w_up[l].astype(BF16), ffn_conv[l], ffn_w_down[l].astype(BF16))

    ffn_specs = [_resident((1, d)), _resident((d, 2 * f)), _resident((3, f)), _resident((f, d))]
    ffn_carry = pltpu.VMEM((CARRY_ROWS, f), F32)

    layer_a = _dense_call(
        _layer_a_kernel, "layer_a", grid,
        [_row_tile(d), _resident((1, d)), _resident((d, 3 * d)), _resident((3, d)), _resident((d, d))] + ffn_specs,
        _row_tile(d), x_shape, [pltpu.VMEM((CARRY_ROWS, d), F32), ffn_carry])
    for l in range(n_a):
        x = layer_a(x, row(attn_norm[l]), a_w_in[l].astype(BF16), a_conv[l], a_w_out[l].astype(BF16),
                    *ffn_operands(l))

    w_f = w_kvf[:, 2 * d:]
    pad = LANES - 3 * N_HEADS
    w_f3 = jnp.pad(jnp.concatenate([w_f, w_f, w_f], axis=1), ((0, 0), (0, pad))).astype(BF16)
    b_f3 = jnp.pad(jnp.concatenate([b_f, b_f, b_f]), (0, pad)).reshape(1, LANES)
    tri = (jnp.arange(ROW_TILE)[:, None] >= jnp.arange(ROW_TILE)[None, :]).astype(BF16)
    n_kv_tiles = s_len // ATTN_TK
    kcat, vt, cp = _dense_call(
        _kv_kernel, "kv_proj", grid,
        [_row_tile(d), _resident((1, d)), _resident((d, d)), _resident((d, d)), _resident((d, LANES)),
         _resident((1, LANES)), _resident((1, d)), _resident((2 * d, LANES)), _resident((2 * LANES, d)),
         _resident((ROW_TILE, ROW_TILE)), _resident((LANES, d)), _resident((1, d))],
        [_row_tile(2 * d), pl.BlockSpec((None, None, d, ATTN_TK), lambda b, s: (b, s, 0, 0)), _row_tile(LANES)],
        [jax.ShapeDtypeStruct((bsz, s_len, 2 * d), BF16),
         jax.ShapeDtypeStruct((bsz, n_kv_tiles, d, ATTN_TK), BF16),
         jax.ShapeDtypeStruct((bsz, s_len, LANES), BF16)],
        [pltpu.VMEM((CARRY_ROWS, LANES), F32)],
    )(x, row(kv_norm), w_kvf[:, :d].astype(BF16), w_kvf[:, d:2 * d].astype(BF16), w_f3, b_f3,
      row(jnp.tile(k_norm, N_HEADS)), red, bcast, tri, place_k, const_k)

    q_proj = _dense_call(
        _q_kernel, "q_proj", grid,
        [_row_tile(d), _resident((1, d)), _resident((d, d)), _resident((d, d)), _resident((1, d)),
         _row_tile(LANES), _resident((2 * d, LANES)), _resident((2 * LANES, d)),
         _resident((LANES, d)), _resident((1, d))],
        [_row_tile(2 * d), _row_tile(d)],
        [jax.ShapeDtypeStruct((bsz, s_len, 2 * d), BF16), x_shape])

    n_pairs = d // PAIR
    attention = pl.pallas_call(
        _attn_kernel, name="fox_attention", grid=(bsz, n_pairs, s_len // ATTN_TQ),
        in_specs=[pl.BlockSpec((None, ATTN_TQ, 2 * PAIR), lambda b, p, i: (b, i, p)),
                  pl.BlockSpec((None, s_len, 2 * PAIR), lambda b, p, i: (b, 0, p)),
                  pl.BlockSpec((None, n_kv_tiles, PAIR, ATTN_TK), lambda b, p, i: (b, 0, p, 0)),
                  pl.BlockSpec((None, ATTN_TQ, PAIR), lambda b, p, i: (b, i, p))],
        out_specs=pl.BlockSpec((None, ATTN_TQ, PAIR), lambda b, p, i: (b, i, p)),
        out_shape=jax.ShapeDtypeStruct((bsz, s_len, d), BF16),
        scratch_shapes=[pltpu.VMEM((2, 2, ATTN_TK, ATTN_TQ), F32), pltpu.VMEM((2, 1, ATTN_TQ), F32),
                        pltpu.VMEM((2, PAIR, ATTN_TQ), F32)],
        compiler_params=pltpu.CompilerParams(
            dimension_semantics=("arbitrary", "arbitrary", "arbitrary"), vmem_limit_bytes=VMEM_LIMIT))

    layer_b = _dense_call(
        _layer_b_kernel, "layer_b", grid,
        [_row_tile(d), _row_tile(d), _resident((d, d))] + ffn_specs,
        _row_tile(d), x_shape, [ffn_carry])

    scale = HEAD_DIM ** -0.5 * LOG2_E
    for l in range(n_a, depth):
        j = l - n_a
        qcat, gate = q_proj(x, row(attn_norm[l]), b_w_qg[j][:, :d].astype(BF16), b_w_qg[j][:, d:].astype(BF16),
                            row(jnp.tile(q_norm[j], N_HEADS)) * scale, cp, red, bcast, place_q, const_q)
        og = attention(qcat, kcat, vt, gate)
        x = layer_b(x, og, b_w_out[j].astype(BF16), *ffn_operands(l))
    return x
```

```python
import functools

import jax
import jax.numpy as jnp
from jax import lax
from jax.experimental import pallas as pl
from jax.experimental.pallas import tpu as pltpu

F32 = jnp.float32
BF16 = jnp.bfloat16

EPS = 1e-6
LOG2_E = 1.4426950408889634
N_HEADS = 16
HEAD_DIM = 64
LANES = 128
PAIR = 2 * HEAD_DIM
CARRY_ROWS = 8

ROW_TILE = 512
CHUNK = 256
ATTN_TQ = 512
ATTN_TK = 512
VMEM_LIMIT = 56 * 1024 * 1024


def _dot(a, b):
    return jnp.dot(a, b, preferred_element_type=F32)


def _rms(x, g):
    return x * lax.rsqrt(jnp.mean(x * x, axis=-1, keepdims=True) + EPS) * g


def _split2(x):
    hi = x.astype(BF16)
    lo = (x - hi.astype(F32)).astype(BF16)
    return hi, lo


def _split3(x):
    p1 = x.astype(BF16)
    r1 = x - p1.astype(F32)
    p2 = r1.astype(BF16)
    p3 = (r1 - p2.astype(F32)).astype(BF16)
    return p1, p2, p3


def _head_rms(q, red_ref, bcast_ref):
    sq_hi, sq_lo = _split2(q * q)
    ms = _dot(jnp.concatenate([sq_hi, sq_lo], axis=1), red_ref[...])
    r_hi, r_lo = _split2(lax.rsqrt(ms + EPS))
    return q * _dot(jnp.concatenate([r_hi, r_lo], axis=1), bcast_ref[...])


def _shift_down(a, first_row):
    row = lax.broadcasted_iota(jnp.int32, a.shape, 0)
    return jnp.where(row == 0, first_row, pltpu.roll(a, 1, axis=0))


def _causal_conv3(u, w, carry_ref, lo, hi):
    rows = u.shape[0]
    old = carry_ref[:, lo:hi]
    carry_ref[:, lo:hi] = u[rows - CARRY_ROWS:, :]
    u1 = _shift_down(u, old[CARRY_ROWS - 1:CARRY_ROWS, :])
    u2 = _shift_down(u1, old[CARRY_ROWS - 2:CARRY_ROWS - 1, :])
    return w[2:3, :] * u + w[1:2, :] * u1 + w[0:1, :] * u2


def _mixer(x, gn_ref, w_in_ref, cw_ref, w_out_ref, carry_ref):
    d = x.shape[1]
    xn = _rms(x, gn_ref[...]).astype(BF16)
    acc = jnp.zeros(x.shape, F32)
    for j in range(d // CHUNK):
        lo, hi = j * CHUNK, (j + 1) * CHUNK
        b = _dot(xn, w_in_ref[:, lo:hi])
        c = _dot(xn, w_in_ref[:, d + lo:d + hi])
        h = _dot(xn, w_in_ref[:, 2 * d + lo:2 * d + hi])
        u = _causal_conv3(c * h, cw_ref[:, lo:hi], carry_ref, lo, hi)
        acc = acc + _dot((b * u).astype(BF16), w_out_ref[lo:hi, :])
    return x + acc


def _conv_ffn(x, gn_ref, w_up_ref, cw_ref, w_down_ref, carry_ref):
    f = w_down_ref.shape[0]
    xn = _rms(x, gn_ref[...]).astype(BF16)
    acc = jnp.zeros(x.shape, F32)
    for j in range(f // CHUNK):
        lo, hi = j * CHUNK, (j + 1) * CHUNK
        a = _dot(xn, w_up_ref[:, lo:hi])
        g = _dot(xn, w_up_ref[:, f + lo:f + hi])
        a = _causal_conv3(a, cw_ref[:, lo:hi], carry_ref, lo, hi)
        hid = a * jax.nn.sigmoid(a) * g
        acc = acc + _dot(hid.astype(BF16), w_down_ref[lo:hi, :])
    return x + acc


def _reset_at_sequence_start(*carry_refs):
    @pl.when(pl.program_id(1) == 0)
    def _():
        for ref in carry_refs:
            ref[...] = jnp.zeros(ref.shape, ref.dtype)


def _layer_a_kernel(x_ref, an_ref, w_in_ref, acw_ref, w_out_ref,
                    fn_ref, w_up_ref, fcw_ref, w_down_ref, o_ref,
                    mix_carry, ffn_carry):
    _reset_at_sequence_start(mix_carry, ffn_carry)
    x = _mixer(x_ref[...], an_ref, w_in_ref, acw_ref, w_out_ref, mix_carry)
    o_ref[...] = _conv_ffn(x, fn_ref, w_up_ref, fcw_ref, w_down_ref, ffn_carry)


def _layer_b_kernel(x_ref, og_ref, w_out_ref,
                    fn_ref, w_up_ref, fcw_ref, w_down_ref, o_ref, ffn_carry):
    _reset_at_sequence_start(ffn_carry)
    x = x_ref[...] + _dot(og_ref[...], w_out_ref[...])
    o_ref[...] = _conv_ffn(x, fn_ref, w_up_ref, fcw_ref, w_down_ref, ffn_carry)


def _interleave_pairs(main, extra, out_ref):
    for p in range(main.shape[1] // PAIR):
        out_ref[:, 2 * p * PAIR:(2 * p + 1) * PAIR] = main[:, p * PAIR:(p + 1) * PAIR].astype(out_ref.dtype)
        out_ref[:, (2 * p + 1) * PAIR:(2 * p + 2) * PAIR] = extra[:, p * PAIR:(p + 1) * PAIR].astype(out_ref.dtype)


def _kv_kernel(x_ref, gn_ref, wk_ref, wv_ref, wf_ref, bf_ref, kg_ref,
               red_ref, bcast_ref, tri_ref, place_ref, const_ref,
               kcat_ref, vt_ref, cp_ref, edge_ref, c_carry):
    _reset_at_sequence_start(c_carry)
    h = _rms(x_ref[...], gn_ref[...]).astype(BF16)
    vt_ref[...] = _dot(h, wv_ref[...]).T.astype(vt_ref.dtype)

    f_logit = _dot(h, wf_ref[...]) + bf_ref[...]
    log_f = jnp.minimum(f_logit, 0.0) - jnp.log1p(jnp.exp(-jnp.abs(f_logit)))
    tri = tri_ref[...]
    c = c_carry[0:1, :] + sum(_dot(tri, piece) for piece in _split3(log_f))
    last = c[c.shape[0] - 1:, :]
    c_carry[0:1, :] = last
    edge_row = lax.broadcasted_iota(jnp.int32, edge_ref.shape, 0)
    edge_ref[...] = jnp.where(edge_row == 0, c[0:1, :], jnp.where(edge_row == 1, last, 0.0))
    c1, c2, c3 = _split3(c * LOG2_E)
    lane = lax.broadcasted_iota(jnp.int32, c.shape, 1)
    cp = jnp.where(lane < N_HEADS, c1, jnp.where(lane < 2 * N_HEADS, c2, c3))
    cp_ref[...] = cp

    k = _head_rms(_dot(h, wk_ref[...]), red_ref, bcast_ref) * kg_ref[...]
    extra = _dot(cp, place_ref[...]) + const_ref[...]
    _interleave_pairs(k, extra, kcat_ref)


def _q_kernel(x_ref, gn_ref, wq_ref, wg_ref, qg_ref, cp_ref,
              red_ref, bcast_ref, place_ref, const_ref,
              qcat_ref, gate_ref):
    xn = _rms(x_ref[...], gn_ref[...]).astype(BF16)
    gate_ref[...] = jax.nn.sigmoid(_dot(xn, wg_ref[...]))
    q = _head_rms(_dot(xn, wq_ref[...]), red_ref, bcast_ref) * qg_ref[...]
    extra = _dot(cp_ref[...], place_ref[...]) + const_ref[...]
    _interleave_pairs(q, extra, qcat_ref)


def _attn_kernel(c_first_ref, c_last_ref, floor_ref, q_ref, k_ref, vt_ref, gate_ref, o_ref,
                 s_scr, m_scr, acc_scr):
    tq = q_ref.shape[0]
    b, pair, qi, hh = (pl.program_id(a) for a in range(4))
    n_tiles = pl.num_programs(2)
    base = (b * N_HEADS + 2 * pair + hh) * n_tiles

    c_query = c_first_ref[base + qi]
    decay_floor = floor_ref[0]
    first_tile = lax.fori_loop(
        0, qi, lambda j, n: n + jnp.where(c_query - c_last_ref[base + j] < decay_floor, 1, 0), 0)

    q = q_ref[...]
    lane_q = lax.broadcasted_iota(jnp.int32, (1, q.shape[1]), 1)
    qh = jnp.where((lane_q % PAIR) // HEAD_DIM == hh, q, jnp.zeros_like(q))
    ones = jnp.ones((HEAD_DIM, ATTN_TK), BF16)
    v_rows = pl.ds(pl.multiple_of(hh * HEAD_DIM, HEAD_DIM), HEAD_DIM)

    def produce(j, slot):
        start = pl.multiple_of(j * ATTN_TK, ATTN_TK)
        s_scr[slot] = lax.dot_general(k_ref[pl.ds(start, ATTN_TK), :], qh, (((1,), (1,)), ((), ())),
                                      preferred_element_type=F32)

    def consume(j, slot, on_diagonal):
        s = s_scr[slot]
        if on_diagonal:
            key_pos = lax.broadcasted_iota(jnp.int32, s.shape, 0)
            query_pos = lax.broadcasted_iota(jnp.int32, s.shape, 1)
            s = jnp.where(key_pos <= query_pos, s, -jnp.inf)
        m = m_scr[...]
        m_new = jnp.maximum(m, jnp.max(s, axis=0, keepdims=True))
        p = jnp.exp2(s - m_new).astype(BF16)
        v_aug = jnp.concatenate([vt_ref[j, v_rows, :], ones], axis=0)
        acc_scr[...] = jnp.exp2(m - m_new) * acc_scr[...] + _dot(v_aug, p)
        m_scr[...] = m_new

    m_scr[...] = jnp.full(m_scr.shape, -jnp.inf, F32)
    acc_scr[...] = jnp.zeros(acc_scr.shape, F32)
    produce(first_tile, 0)
    n_full = qi - first_tile

    @pl.loop(0, n_full // 2)
    def _(i):
        j = first_tile + 2 * i
        produce(j + 1, 1)
        consume(j, 0, False)
        produce(j + 2, 0)
        consume(j + 1, 1, False)

    @pl.when(n_full % 2 == 0)
    def _():
        consume(qi, 0, True)

    @pl.when(n_full % 2 == 1)
    def _():
        produce(qi, 1)
        consume(qi - 1, 0, False)
        consume(qi, 1, True)

    o_t = acc_scr[:HEAD_DIM, :] / acc_scr[HEAD_DIM:, :]
    zeros = jnp.zeros_like(o_t)
    o_pair_t = jnp.where(hh == 0, jnp.concatenate([o_t, zeros], axis=0), jnp.concatenate([zeros, o_t], axis=0))
    o = (o_pair_t.T * gate_ref[...]).astype(o_ref.dtype)

    @pl.when(hh == 0)
    def _():
        o_ref[...] = o

    @pl.when(hh == 1)
    def _():
        o_ref[...] += o


def _resident(shape):
    return pl.BlockSpec(shape, lambda *_: (0,) * len(shape), pipeline_mode=pl.Buffered(1))


def _row_tile(cols):
    return pl.BlockSpec((None, ROW_TILE, cols), lambda b, s: (b, s, 0))


def _dense_call(kernel, name, grid, in_specs, out_specs, out_shape, scratch_shapes=()):
    return pl.pallas_call(
        kernel, name=name, grid=grid, in_specs=in_specs, out_specs=out_specs, out_shape=out_shape,
        scratch_shapes=list(scratch_shapes),
        compiler_params=pltpu.CompilerParams(
            dimension_semantics=("arbitrary", "arbitrary"), vmem_limit_bytes=VMEM_LIMIT))


def _placement_constants():
    d = N_HEADS * HEAD_DIM
    head_of_col = jnp.arange(d) // HEAD_DIM
    pos_in_head = jnp.arange(d) % HEAD_DIM
    lane = jnp.arange(LANES)
    red = (head_of_col[:, None] == lane[None, :]).astype(F32) / HEAD_DIM
    red = jnp.concatenate([red, red], axis=0).astype(BF16)
    bcast = (lane[:, None] == head_of_col[None, :]).astype(BF16)
    bcast = jnp.concatenate([bcast, bcast], axis=0)
    piece, head = lane // N_HEADS, lane % N_HEADS
    valid = (piece < 3)[:, None] & (head[:, None] == head_of_col[None, :])
    place_q = (valid & (pos_in_head[None, :] == piece[:, None])).astype(BF16)
    place_k = -(valid & (pos_in_head[None, :] == 3 + piece[:, None])).astype(BF16)
    const_q = ((pos_in_head >= 3) & (pos_in_head < 6)).astype(F32)[None, :]
    const_k = (pos_in_head < 3).astype(F32)[None, :]
    return red, bcast, place_q, const_q, place_k, const_k


def kernel(x, attn_norm, ffn_norm, a_w_in, a_conv, a_w_out, kv_norm, w_kvf, b_f, k_norm,
           b_w_qg, q_norm, b_w_out, ffn_w_up, ffn_conv, ffn_w_down):
    bsz, s_len, d = x.shape
    depth = ffn_w_up.shape[0]
    n_a = a_w_in.shape[0]
    f = ffn_w_down.shape[1]
    assert d == N_HEADS * HEAD_DIM and s_len % ROW_TILE == 0 and s_len % ATTN_TQ == 0
    assert ATTN_TQ == ATTN_TK == ROW_TILE and d % CHUNK == 0 and f % CHUNK == 0

    grid = (bsz, s_len // ROW_TILE)
    x_shape = jax.ShapeDtypeStruct((bsz, s_len, d), F32)
    red, bcast, place_q, const_q, place_k, const_k = _placement_constants()
    row = lambda v: v.reshape(1, -1)

    def ffn_operands(l):
        return (row(ffn_norm[l]), ffn_w_up[l].astype(BF16), ffn_conv[l], ffn_w_down[l].astype(BF16))

    ffn_specs = [_resident((1, d)), _resident((d, 2 * f)), _resident((3, f)), _resident((f, d))]
    ffn_carry = pltpu.VMEM((CARRY_ROWS, f), F32)

    layer_a = _dense_call(
        _layer_a_kernel, "layer_a", grid,
        [_row_tile(d), _resident((1, d)), _resident((d, 3 * d)), _resident((3, d)), _resident((d, d))] + ffn_specs,
        _row_tile(d), x_shape, [pltpu.VMEM((CARRY_ROWS, d), F32), ffn_carry])
    for l in range(n_a):
        x = layer_a(x, row(attn_norm[l]), a_w_in[l].astype(BF16), a_conv[l], a_w_out[l].astype(BF16),
                    *ffn_operands(l))

    w_f = w_kvf[:, 2 * d:]
    pad = LANES - 3 * N_HEADS
    w_f3 = jnp.pad(jnp.concatenate([w_f, w_f, w_f], axis=1), ((0, 0), (0, pad))).astype(BF16)
    b_f3 = jnp.pad(jnp.concatenate([b_f, b_f, b_f]), (0, pad)).reshape(1, LANES)
    tri = (jnp.arange(ROW_TILE)[:, None] >= jnp.arange(ROW_TILE)[None, :]).astype(BF16)
    n_kv_tiles = s_len // ATTN_TK
    kcat, vt, cp, c_edges = _dense_call(
        _kv_kernel, "kv_proj", grid,
        [_row_tile(d), _resident((1, d)), _resident((d, d)), _resident((d, d)), _resident((d, LANES)),
         _resident((1, LANES)), _resident((1, d)), _resident((2 * d, LANES)), _resident((2 * LANES, d)),
         _resident((ROW_TILE, ROW_TILE)), _resident((LANES, d)), _resident((1, d))],
        [_row_tile(2 * d), pl.BlockSpec((None, None, d, ATTN_TK), lambda b, s: (b, s, 0, 0)), _row_tile(LANES),
         pl.BlockSpec((None, None, CARRY_ROWS, LANES), lambda b, s: (b, s, 0, 0))],
        [jax.ShapeDtypeStruct((bsz, s_len, 2 * d), BF16),
         jax.ShapeDtypeStruct((bsz, n_kv_tiles, d, ATTN_TK), BF16),
         jax.ShapeDtypeStruct((bsz, s_len, LANES), BF16),
         jax.ShapeDtypeStruct((bsz, n_kv_tiles, CARRY_ROWS, LANES), F32)],
        [pltpu.VMEM((CARRY_ROWS, LANES), F32)],
    )(x, row(kv_norm), w_kvf[:, :d].astype(BF16), w_kvf[:, d:2 * d].astype(BF16), w_f3, b_f3,
      row(jnp.tile(k_norm, N_HEADS)), red, bcast, tri, place_k, const_k)
    c_first = c_edges[:, :, 0, :N_HEADS].transpose(0, 2, 1).reshape(-1)
    c_last = c_edges[:, :, 1, :N_HEADS].transpose(0, 2, 1).reshape(-1)

    q_proj = _dense_call(
        _q_kernel, "q_proj", grid,
        [_row_tile(d), _resident((1, d)), _resident((d, d)), _resident((d, d)), _resident((1, d)),
         _row_tile(LANES), _resident((2 * d, LANES)), _resident((2 * LANES, d)),
         _resident((LANES, d)), _resident((1, d))],
        [_row_tile(2 * d), _row_tile(d)],
        [jax.ShapeDtypeStruct((bsz, s_len, 2 * d), BF16), x_shape])

    n_pairs = d // PAIR
    attention = pl.pallas_call(
        _attn_kernel, name="fox_attention",
        grid_spec=pltpu.PrefetchScalarGridSpec(
            num_scalar_prefetch=3, grid=(bsz, n_pairs, s_len // ATTN_TQ, 2),
            in_specs=[pl.BlockSpec((None, ATTN_TQ, 2 * PAIR), lambda b, p, i, h, *_: (b, i, p)),
                      pl.BlockSpec((None, s_len, 2 * PAIR), lambda b, p, i, h, *_: (b, 0, p)),
                      pl.BlockSpec((None, n_kv_tiles, PAIR, ATTN_TK), lambda b, p, i, h, *_: (b, 0, p, 0)),
                      pl.BlockSpec((None, ATTN_TQ, PAIR), lambda b, p, i, h, *_: (b, i, p))],
            out_specs=pl.BlockSpec((None, ATTN_TQ, PAIR), lambda b, p, i, h, *_: (b, i, p)),
            scratch_shapes=[pltpu.VMEM((2, ATTN_TK, ATTN_TQ), F32), pltpu.VMEM((1, ATTN_TQ), F32),
                            pltpu.VMEM((PAIR, ATTN_TQ), F32)]),
        out_shape=jax.ShapeDtypeStruct((bsz, s_len, d), BF16),
        compiler_params=pltpu.CompilerParams(
            dimension_semantics=("arbitrary",) * 4, vmem_limit_bytes=VMEM_LIMIT))
    k_gain = jnp.max(jnp.abs(k_norm))

    def decay_floor(q_gain):
        return (-(2.04 * HEAD_DIM ** 0.5 * jnp.max(jnp.abs(q_gain)) * k_gain + 104.0)).reshape(1).astype(F32)

    layer_b = _dense_call(
        _layer_b_kernel, "layer_b", grid,
        [_row_tile(d), _row_tile(d), _resident((d, d))] + ffn_specs,
        _row_tile(d), x_shape, [ffn_carry])

    scale = HEAD_DIM ** -0.5 * LOG2_E
    for l in range(n_a, depth):
        j = l - n_a
        qcat, gate = q_proj(x, row(attn_norm[l]), b_w_qg[j][:, :d].astype(BF16), b_w_qg[j][:, d:].astype(BF16),
                            row(jnp.tile(q_norm[j], N_HEADS)) * scale, cp, red, bcast, place_q, const_q)
        og = attention(c_first, c_last, decay_floor(q_norm[j]), qcat, kcat, vt, gate)
        x = layer_b(x, og, b_w_out[j].astype(BF16), *ffn_operands(l))
    return x
```

```python
import functools

import jax
import jax.numpy as jnp
from jax import lax
from jax.experimental import pallas as pl
from jax.experimental.pallas import tpu as pltpu

F32 = jnp.float32
BF16 = jnp.bfloat16

EPS = 1e-6
LOG2_E = 1.4426950408889634
N_HEADS = 16
HEAD_DIM = 64
LANES = 128
PAIR = 2 * HEAD_DIM
CARRY_ROWS = 8

ROW_TILE = 512
CHUNK = 256
ATTN_TQ = 512
ATTN_TK = 512
SCORE_SLOTS = 4
VMEM_LIMIT = 56 * 1024 * 1024


def _dot(a, b):
    return jnp.dot(a, b, preferred_element_type=F32)


def _rms(x, g):
    return x * lax.rsqrt(jnp.mean(x * x, axis=-1, keepdims=True) + EPS) * g


def _split2(x):
    hi = x.astype(BF16)
    lo = (x - hi.astype(F32)).astype(BF16)
    return hi, lo


def _split3(x):
    p1 = x.astype(BF16)
    r1 = x - p1.astype(F32)
    p2 = r1.astype(BF16)
    p3 = (r1 - p2.astype(F32)).astype(BF16)
    return p1, p2, p3


def _head_rms(q, red_ref, bcast_ref):
    sq_hi, sq_lo = _split2(q * q)
    ms = _dot(jnp.concatenate([sq_hi, sq_lo], axis=1), red_ref[...])
    r_hi, r_lo = _split2(lax.rsqrt(ms + EPS))
    return q * _dot(jnp.concatenate([r_hi, r_lo], axis=1), bcast_ref[...])


def _shift_down(a, first_row):
    row = lax.broadcasted_iota(jnp.int32, a.shape, 0)
    return jnp.where(row == 0, first_row, pltpu.roll(a, 1, axis=0))


def _causal_conv3(u, w, carry_ref, lo, hi):
    rows = u.shape[0]
    old = carry_ref[:, lo:hi]
    carry_ref[:, lo:hi] = u[rows - CARRY_ROWS:, :]
    u1 = _shift_down(u, old[CARRY_ROWS - 1:CARRY_ROWS, :])
    u2 = _shift_down(u1, old[CARRY_ROWS - 2:CARRY_ROWS - 1, :])
    return w[2:3, :] * u + w[1:2, :] * u1 + w[0:1, :] * u2


def _mixer(x, gn_ref, w_in_ref, cw_ref, w_out_ref, carry_ref):
    d = x.shape[1]
    xn = _rms(x, gn_ref[...]).astype(BF16)
    acc = jnp.zeros(x.shape, F32)
    for j in range(d // CHUNK):
        lo, hi = j * CHUNK, (j + 1) * CHUNK
        b = _dot(xn, w_in_ref[:, lo:hi])
        c = _dot(xn, w_in_ref[:, d + lo:d + hi])
        h = _dot(xn, w_in_ref[:, 2 * d + lo:2 * d + hi])
        u = _causal_conv3(c * h, cw_ref[:, lo:hi], carry_ref, lo, hi)
        acc = acc + _dot((b * u).astype(BF16), w_out_ref[lo:hi, :])
    return x + acc


def _conv_ffn(x, gn_ref, w_up_ref, cw_ref, w_down_ref, carry_ref):
    f = w_down_ref.shape[0]
    xn = _rms(x, gn_ref[...]).astype(BF16)
    acc = jnp.zeros(x.shape, F32)
    for j in range(f // CHUNK):
        lo, hi = j * CHUNK, (j + 1) * CHUNK
        a = _dot(xn, w_up_ref[:, lo:hi])
        g = _dot(xn, w_up_ref[:, f + lo:f + hi])
        a = _causal_conv3(a, cw_ref[:, lo:hi], carry_ref, lo, hi)
        hid = a * jax.nn.sigmoid(a) * g
        acc = acc + _dot(hid.astype(BF16), w_down_ref[lo:hi, :])
    return x + acc


def _reset_at_sequence_start(*carry_refs):
    @pl.when(pl.program_id(1) == 0)
    def _():
        for ref in carry_refs:
            ref[...] = jnp.zeros(ref.shape, ref.dtype)


def _layer_a_kernel(x_ref, an_ref, w_in_ref, acw_ref, w_out_ref,
                    fn_ref, w_up_ref, fcw_ref, w_down_ref, o_ref,
                    mix_carry, ffn_carry):
    _reset_at_sequence_start(mix_carry, ffn_carry)
    x = _mixer(x_ref[...], an_ref, w_in_ref, acw_ref, w_out_ref, mix_carry)
    o_ref[...] = _conv_ffn(x, fn_ref, w_up_ref, fcw_ref, w_down_ref, ffn_carry)


def _layer_b_kernel(x_ref, og_ref, w_out_ref,
                    fn_ref, w_up_ref, fcw_ref, w_down_ref, o_ref, ffn_carry):
    _reset_at_sequence_start(ffn_carry)
    x = x_ref[...] + _dot(og_ref[...], w_out_ref[...])
    o_ref[...] = _conv_ffn(x, fn_ref, w_up_ref, fcw_ref, w_down_ref, ffn_carry)


def _interleave_pairs(main, extra, out_ref):
    for p in range(main.shape[1] // PAIR):
        out_ref[:, 2 * p * PAIR:(2 * p + 1) * PAIR] = main[:, p * PAIR:(p + 1) * PAIR].astype(out_ref.dtype)
        out_ref[:, (2 * p + 1) * PAIR:(2 * p + 2) * PAIR] = extra[:, p * PAIR:(p + 1) * PAIR].astype(out_ref.dtype)


def _kv_kernel(x_ref, gn_ref, wk_ref, wv_ref, wf_ref, bf_ref, kg_ref,
               red_ref, bcast_ref, tri_ref, place_ref, const_ref,
               kcat_ref, vt_ref, cp_ref, edge_ref, c_carry):
    _reset_at_sequence_start(c_carry)
    h = _rms(x_ref[...], gn_ref[...]).astype(BF16)
    vt_ref[...] = _dot(h, wv_ref[...]).T.astype(vt_ref.dtype)

    f_logit = _dot(h, wf_ref[...]) + bf_ref[...]
    log_f = jnp.minimum(f_logit, 0.0) - jnp.log1p(jnp.exp(-jnp.abs(f_logit)))
    tri = tri_ref[...]
    c = c_carry[0:1, :] + sum(_dot(tri, piece) for piece in _split3(log_f))
    last = c[c.shape[0] - 1:, :]
    c_carry[0:1, :] = last
    edge_row = lax.broadcasted_iota(jnp.int32, edge_ref.shape, 0)
    edge_ref[...] = jnp.where(edge_row == 0, c[0:1, :], jnp.where(edge_row == 1, last, 0.0))
    c1, c2, c3 = _split3(c * LOG2_E)
    lane = lax.broadcasted_iota(jnp.int32, c.shape, 1)
    cp = jnp.where(lane < N_HEADS, c1, jnp.where(lane < 2 * N_HEADS, c2, c3))
    cp_ref[...] = cp

    k = _head_rms(_dot(h, wk_ref[...]), red_ref, bcast_ref) * kg_ref[...]
    extra = _dot(cp, place_ref[...]) + const_ref[...]
    _interleave_pairs(k, extra, kcat_ref)


def _q_kernel(x_ref, gn_ref, wq_ref, wg_ref, qg_ref, cp_ref,
              red_ref, bcast_ref, place_ref, const_ref,
              qcat_ref, gate_ref):
    xn = _rms(x_ref[...], gn_ref[...]).astype(BF16)
    gate_ref[...] = jax.nn.sigmoid(_dot(xn, wg_ref[...]))
    q = _head_rms(_dot(xn, wq_ref[...]), red_ref, bcast_ref) * qg_ref[...]
    extra = _dot(cp_ref[...], place_ref[...]) + const_ref[...]
    _interleave_pairs(q, extra, qcat_ref)


def _attn_kernel(c_first_ref, c_last_ref, floor_ref, q_ref, k_ref, vt_ref, gate_ref, o_ref,
                 qh_scr, s_scr, m_scr, acc_scr):
    b, pair, qi = (pl.program_id(a) for a in range(3))
    n_tiles = pl.num_programs(2)
    decay_floor = floor_ref[0]

    def first_needed_tile(head):
        base = (b * N_HEADS + 2 * pair + head) * n_tiles
        c_query = c_first_ref[base + qi]
        return lax.fori_loop(
            0, qi, lambda j, n: n + jnp.where(c_query - c_last_ref[base + j] < decay_floor, 1, 0), 0)

    first0, first1 = first_needed_tile(0), first_needed_tile(1)
    odd = (first0 + first1) % 2
    first0, first1 = (first0 - jnp.where((odd == 1) & (first0 > 0), 1, 0),
                      first1 - jnp.where((odd == 1) & (first0 == 0), 1, 0))
    n0, n1 = qi - first0, qi - first1
    n_full = n0 + n1

    def item(u):
        in0, in1 = u < n0, u < n_full
        return jnp.where(in0, 0, jnp.where(in1, 1, 0)), jnp.where(in0, first0 + u, jnp.where(in1, first1 + u - n0, qi))

    q = q_ref[...]
    lane_q = lax.broadcasted_iota(jnp.int32, (1, q.shape[1]), 1)
    for head in range(2):
        qh_scr[head] = jnp.where((lane_q % PAIR) // HEAD_DIM == head, q, jnp.zeros_like(q))
    ones = jnp.ones((HEAD_DIM, ATTN_TK), BF16)

    def produce(head, j, slot):
        start = pl.multiple_of(j * ATTN_TK, ATTN_TK)
        s_scr[slot] = lax.dot_general(k_ref[pl.ds(start, ATTN_TK), :], qh_scr[head], (((1,), (1,)), ((), ())),
                                      preferred_element_type=F32)

    def consume(head, j, slot, on_diagonal):
        s = s_scr[slot]
        if on_diagonal:
            key_pos = lax.broadcasted_iota(jnp.int32, s.shape, 0)
            query_pos = lax.broadcasted_iota(jnp.int32, s.shape, 1)
            s = jnp.where(key_pos <= query_pos, s, -jnp.inf)
        m = m_scr[head]
        m_new = jnp.maximum(m, jnp.max(s, axis=0, keepdims=True))
        p = jnp.exp2(s - m_new).astype(BF16)
        v_rows = pl.ds(pl.multiple_of(head * HEAD_DIM, HEAD_DIM), HEAD_DIM)
        v_aug = jnp.concatenate([vt_ref[j, v_rows, :], ones], axis=0)
        acc_scr[head] = jnp.exp2(m - m_new) * acc_scr[head] + _dot(v_aug, p)
        m_scr[head] = m_new

    def pipelined(u, n_items):
        for k in range(n_items):
            produce(*item(u + k + 1), (k + 1) % n_items)
            consume(*item(u + k), k, False)

    m_scr[...] = jnp.full(m_scr.shape, -jnp.inf, F32)
    acc_scr[...] = jnp.zeros(acc_scr.shape, F32)
    produce(*item(0), 0)

    @pl.loop(0, n_full // SCORE_SLOTS)
    def _(i):
        pipelined(SCORE_SLOTS * i, SCORE_SLOTS)

    @pl.when(n_full % SCORE_SLOTS == 2)
    def _():
        pipelined(n_full - 2, 2)

    produce(1, qi, 1)
    consume(0, qi, 0, True)
    consume(1, qi, 1, True)

    o_t = jnp.concatenate([acc_scr[head, :HEAD_DIM, :] / acc_scr[head, HEAD_DIM:, :] for head in range(2)], axis=0)
    o_ref[...] = (o_t.T * gate_ref[...]).astype(o_ref.dtype)


def _resident(shape):
    return pl.BlockSpec(shape, lambda *_: (0,) * len(shape), pipeline_mode=pl.Buffered(1))


def _row_tile(cols):
    return pl.BlockSpec((None, ROW_TILE, cols), lambda b, s: (b, s, 0))


def _dense_call(kernel, name, grid, in_specs, out_specs, out_shape, scratch_shapes=()):
    return pl.pallas_call(
        kernel, name=name, grid=grid, in_specs=in_specs, out_specs=out_specs, out_shape=out_shape,
        scratch_shapes=list(scratch_shapes),
        compiler_params=pltpu.CompilerParams(
            dimension_semantics=("arbitrary", "arbitrary"), vmem_limit_bytes=VMEM_LIMIT))


def _placement_constants():
    d = N_HEADS * HEAD_DIM
    head_of_col = jnp.arange(d) // HEAD_DIM
    pos_in_head = jnp.arange(d) % HEAD_DIM
    lane = jnp.arange(LANES)
    red = (head_of_col[:, None] == lane[None, :]).astype(F32) / HEAD_DIM
    red = jnp.concatenate([red, red], axis=0).astype(BF16)
    bcast = (lane[:, None] == head_of_col[None, :]).astype(BF16)
    bcast = jnp.concatenate([bcast, bcast], axis=0)
    piece, head = lane // N_HEADS, lane % N_HEADS
    valid = (piece < 3)[:, None] & (head[:, None] == head_of_col[None, :])
    place_q = (valid & (pos_in_head[None, :] == piece[:, None])).astype(BF16)
    place_k = -(valid & (pos_in_head[None, :] == 3 + piece[:, None])).astype(BF16)
    const_q = ((pos_in_head >= 3) & (pos_in_head < 6)).astype(F32)[None, :]
    const_k = (pos_in_head < 3).astype(F32)[None, :]
    return red, bcast, place_q, const_q, place_k, const_k


def kernel(x, attn_norm, ffn_norm, a_w_in, a_conv, a_w_out, kv_norm, w_kvf, b_f, k_norm,
           b_w_qg, q_norm, b_w_out, ffn_w_up, ffn_conv, ffn_w_down):
    bsz, s_len, d = x.shape
    depth = ffn_w_up.shape[0]
    n_a = a_w_in.shape[0]
    f = ffn_w_down.shape[1]
    assert d == N_HEADS * HEAD_DIM and s_len % ROW_TILE == 0 and s_len % ATTN_TQ == 0
    assert ATTN_TQ == ATTN_TK == ROW_TILE and d % CHUNK == 0 and f % CHUNK == 0

    grid = (bsz, s_len // ROW_TILE)
    x_shape = jax.ShapeDtypeStruct((bsz, s_len, d), F32)
    red, bcast, place_q, const_q, place_k, const_k = _placement_constants()
    row = lambda v: v.reshape(1, -1)

    def ffn_operands(l):
        return (row(ffn_norm[l]), ffn_w_up[l].astype(BF16), ffn_conv[l], ffn_w_down[l].astype(BF16))

    ffn_specs = [_resident((1, d)), _resident((d, 2 * f)), _resident((3, f)), _resident((f, d))]
    ffn_carry = pltpu.VMEM((CARRY_ROWS, f), F32)

    layer_a = _dense_call(
        _layer_a_kernel, "layer_a", grid,
        [_row_tile(d), _resident((1, d)), _resident((d, 3 * d)), _resident((3, d)), _resident((d, d))] + ffn_specs,
        _row_tile(d), x_shape, [pltpu.VMEM((CARRY_ROWS, d), F32), ffn_carry])
    for l in range(n_a):
        x = layer_a(x, row(attn_norm[l]), a_w_in[l].astype(BF16), a_conv[l], a_w_out[l].astype(BF16),
                    *ffn_operands(l))

    w_f = w_kvf[:, 2 * d:]
    pad = LANES - 3 * N_HEADS
    w_f3 = jnp.pad(jnp.concatenate([w_f, w_f, w_f], axis=1), ((0, 0), (0, pad))).astype(BF16)
    b_f3 = jnp.pad(jnp.concatenate([b_f, b_f, b_f]), (0, pad)).reshape(1, LANES)
    tri = (jnp.arange(ROW_TILE)[:, None] >= jnp.arange(ROW_TILE)[None, :]).astype(BF16)
    n_kv_tiles = s_len // ATTN_TK
    kcat, vt, cp, c_edges = _dense_call(
        _kv_kernel, "kv_proj", grid,
        [_row_tile(d), _resident((1, d)), _resident((d, d)), _resident((d, d)), _resident((d, LANES)),
         _resident((1, LANES)), _resident((1, d)), _resident((2 * d, LANES)), _resident((2 * LANES, d)),
         _resident((ROW_TILE, ROW_TILE)), _resident((LANES, d)), _resident((1, d))],
        [_row_tile(2 * d), pl.BlockSpec((None, None, d, ATTN_TK), lambda b, s: (b, s, 0, 0)), _row_tile(LANES),
         pl.BlockSpec((None, None, CARRY_ROWS, LANES), lambda b, s: (b, s, 0, 0))],
        [jax.ShapeDtypeStruct((bsz, s_len, 2 * d), BF16),
         jax.ShapeDtypeStruct((bsz, n_kv_tiles, d, ATTN_TK), BF16),
         jax.ShapeDtypeStruct((bsz, s_len, LANES), BF16),
         jax.ShapeDtypeStruct((bsz, n_kv_tiles, CARRY_ROWS, LANES), F32)],
        [pltpu.VMEM((CARRY_ROWS, LANES), F32)],
    )(x, row(kv_norm), w_kvf[:, :d].astype(BF16), w_kvf[:, d:2 * d].astype(BF16), w_f3, b_f3,
      row(jnp.tile(k_norm, N_HEADS)), red, bcast, tri, place_k, const_k)
    c_first = c_edges[:, :, 0, :N_HEADS].transpose(0, 2, 1).reshape(-1)
    c_last = c_edges[:, :, 1, :N_HEADS].transpose(0, 2, 1).reshape(-1)

    q_proj = _dense_call(
        _q_kernel, "q_proj", grid,
        [_row_tile(d), _resident((1, d)), _resident((d, d)), _resident((d, d)), _resident((1, d)),
         _row_tile(LANES), _resident((2 * d, LANES)), _resident((2 * LANES, d)),
         _resident((LANES, d)), _resident((1, d))],
        [_row_tile(2 * d), _row_tile(d)],
        [jax.ShapeDtypeStruct((bsz, s_len, 2 * d), BF16), x_shape])

    n_pairs = d // PAIR
    attention = pl.pallas_call(
        _attn_kernel, name="fox_attention",
        grid_spec=pltpu.PrefetchScalarGridSpec(
            num_scalar_prefetch=3, grid=(bsz, n_pairs, s_len // ATTN_TQ),
            in_specs=[pl.BlockSpec((None, ATTN_TQ, 2 * PAIR), lambda b, p, i, *_: (b, i, p)),
                      pl.BlockSpec((None, s_len, 2 * PAIR), lambda b, p, i, *_: (b, 0, p)),
                      pl.BlockSpec((None, n_kv_tiles, PAIR, ATTN_TK), lambda b, p, i, *_: (b, 0, p, 0)),
                      pl.BlockSpec((None, ATTN_TQ, PAIR), lambda b, p, i, *_: (b, i, p))],
            out_specs=pl.BlockSpec((None, ATTN_TQ, PAIR), lambda b, p, i, *_: (b, i, p)),
            scratch_shapes=[pltpu.VMEM((2, ATTN_TQ, 2 * PAIR), BF16),
                            pltpu.VMEM((SCORE_SLOTS, ATTN_TK, ATTN_TQ), F32),
                            pltpu.VMEM((2, 1, ATTN_TQ), F32), pltpu.VMEM((2, PAIR, ATTN_TQ), F32)]),
        out_shape=jax.ShapeDtypeStruct((bsz, s_len, d), BF16),
        compiler_params=pltpu.CompilerParams(
            dimension_semantics=("arbitrary",) * 3, vmem_limit_bytes=VMEM_LIMIT))
    k_gain = jnp.max(jnp.abs(k_norm))

    def decay_floor(q_gain):
        return (-(2.04 * HEAD_DIM ** 0.5 * jnp.max(jnp.abs(q_gain)) * k_gain + 104.0)).reshape(1).astype(F32)

    layer_b = _dense_call(
        _layer_b_kernel, "layer_b", grid,
        [_row_tile(d), _row_tile(d), _resident((d, d))] + ffn_specs,
        _row_tile(d), x_shape, [ffn_carry])

    scale = HEAD_DIM ** -0.5 * LOG2_E
    for l in range(n_a, depth):
        j = l - n_a
        qcat, gate = q_proj(x, row(attn_norm[l]), b_w_qg[j][:, :d].astype(BF16), b_w_qg[j][:, d:].astype(BF16),
                            row(jnp.tile(q_norm[j], N_HEADS)) * scale, cp, red, bcast, place_q, const_q)
        og = attention(c_first, c_last, decay_floor(q_norm[j]), qcat, kcat, vt, gate)
        x = layer_b(x, og, b_w_out[j].astype(BF16), *ffn_operands(l))
    return x
```

```python
import functools

import jax
import jax.numpy as jnp
from jax import lax
from jax.experimental import pallas as pl
from jax.experimental.pallas import tpu as pltpu

F32 = jnp.float32
BF16 = jnp.bfloat16

EPS = 1e-6
LOG2_E = 1.4426950408889634
N_HEADS = 16
HEAD_DIM = 64
LANES = 128
PAIR = 2 * HEAD_DIM
CARRY_ROWS = 8

ROW_TILE = 512
CHUNK = 256
ATTN_TQ = 512
ATTN_TK = 512
SCORE_SLOTS = 4
VMEM_LIMIT = 56 * 1024 * 1024


def _dot(a, b):
    return jnp.dot(a, b, preferred_element_type=F32)


def _rms(x, g):
    return x * lax.rsqrt(jnp.mean(x * x, axis=-1, keepdims=True) + EPS) * g


def _split2(x):
    hi = x.astype(BF16)
    lo = (x - hi.astype(F32)).astype(BF16)
    return hi, lo


def _split3(x):
    p1 = x.astype(BF16)
    r1 = x - p1.astype(F32)
    p2 = r1.astype(BF16)
    p3 = (r1 - p2.astype(F32)).astype(BF16)
    return p1, p2, p3


def _head_rms(q, red_ref, bcast_ref):
    sq_hi, sq_lo = _split2(q * q)
    ms = _dot(jnp.concatenate([sq_hi, sq_lo], axis=1), red_ref[...])
    r_hi, r_lo = _split2(lax.rsqrt(ms + EPS))
    return q * _dot(jnp.concatenate([r_hi, r_lo], axis=1), bcast_ref[...])


def _shift_down(a, first_row):
    row = lax.broadcasted_iota(jnp.int32, a.shape, 0)
    return jnp.where(row == 0, first_row, pltpu.roll(a, 1, axis=0))


def _causal_conv3(u, w, carry_ref, lo, hi):
    rows = u.shape[0]
    old = carry_ref[:, lo:hi]
    carry_ref[:, lo:hi] = u[rows - CARRY_ROWS:, :]
    u1 = _shift_down(u, old[CARRY_ROWS - 1:CARRY_ROWS, :])
    u2 = _shift_down(u1, old[CARRY_ROWS - 2:CARRY_ROWS - 1, :])
    return w[2:3, :] * u + w[1:2, :] * u1 + w[0:1, :] * u2


def _mixer(x, gn_ref, w_in_ref, cw_ref, w_out_ref, carry_ref):
    d = x.shape[1]
    xn = _rms(x, gn_ref[...]).astype(BF16)

    def project(j):
        return tuple(_dot(xn, w_in_ref[:, part * d + j * CHUNK:part * d + (j + 1) * CHUNK]) for part in range(3))

    n_chunks = d // CHUNK
    acc = jnp.zeros(x.shape, F32)
    ahead = project(0)
    for j in range(n_chunks):
        lo, hi = j * CHUNK, (j + 1) * CHUNK
        b, c, h = ahead
        if j + 1 < n_chunks:
            ahead = project(j + 1)
        u = _causal_conv3(c * h, cw_ref[:, lo:hi], carry_ref, lo, hi)
        acc = acc + _dot((b * u).astype(BF16), w_out_ref[lo:hi, :])
    return x + acc


def _conv_ffn(x, gn_ref, w_up_ref, cw_ref, w_down_ref, carry_ref):
    f = w_down_ref.shape[0]
    xn = _rms(x, gn_ref[...]).astype(BF16)

    def project(j):
        return tuple(_dot(xn, w_up_ref[:, part * f + j * CHUNK:part * f + (j + 1) * CHUNK]) for part in range(2))

    n_chunks = f // CHUNK
    acc = jnp.zeros(x.shape, F32)
    ahead = project(0)
    for j in range(n_chunks):
        lo, hi = j * CHUNK, (j + 1) * CHUNK
        a, g = ahead
        if j + 1 < n_chunks:
            ahead = project(j + 1)
        a = _causal_conv3(a, cw_ref[:, lo:hi], carry_ref, lo, hi)
        hid = a * jax.nn.sigmoid(a) * g
        acc = acc + _dot(hid.astype(BF16), w_down_ref[lo:hi, :])
    return x + acc


def _row_halves(rows):
    return [pl.ds(0, rows // 2), pl.ds(rows // 2, rows // 2)]


def _reset_at_sequence_start(*carry_refs):
    @pl.when(pl.program_id(1) == 0)
    def _():
        for ref in carry_refs:
            ref[...] = jnp.zeros(ref.shape, ref.dtype)


def _layer_a_kernel(x_ref, an_ref, w_in_ref, acw_ref, w_out_ref,
                    fn_ref, w_up_ref, fcw_ref, w_down_ref, o_ref,
                    mix_carry, ffn_carry):
    _reset_at_sequence_start(mix_carry, ffn_carry)
    halves = _row_halves(x_ref.shape[0])
    mixed = [_mixer(x_ref[rows, :], an_ref, w_in_ref, acw_ref, w_out_ref, mix_carry) for rows in halves]
    for rows, x in zip(halves, mixed):
        o_ref[rows, :] = _conv_ffn(x, fn_ref, w_up_ref, fcw_ref, w_down_ref, ffn_carry)


def _layer_b_kernel(x_ref, og_ref, w_out_ref,
                    fn_ref, w_up_ref, fcw_ref, w_down_ref, o_ref, ffn_carry):
    _reset_at_sequence_start(ffn_carry)
    halves = _row_halves(x_ref.shape[0])
    attended = [x_ref[rows, :] + _dot(og_ref[rows, :], w_out_ref[...]) for rows in halves]
    for rows, x in zip(halves, attended):
        o_ref[rows, :] = _conv_ffn(x, fn_ref, w_up_ref, fcw_ref, w_down_ref, ffn_carry)


def _interleave_pairs(main, extra, out_ref):
    for p in range(main.shape[1] // PAIR):
        out_ref[:, 2 * p * PAIR:(2 * p + 1) * PAIR] = main[:, p * PAIR:(p + 1) * PAIR].astype(out_ref.dtype)
        out_ref[:, (2 * p + 1) * PAIR:(2 * p + 2) * PAIR] = extra[:, p * PAIR:(p + 1) * PAIR].astype(out_ref.dtype)


def _kv_kernel(x_ref, gn_ref, wk_ref, wv_ref, wf_ref, bf_ref, kg_ref,
               red_ref, bcast_ref, tri_ref, place_ref, const_ref,
               kcat_ref, vt_ref, cp_ref, edge_ref, c_carry):
    _reset_at_sequence_start(c_carry)
    h = _rms(x_ref[...], gn_ref[...]).astype(BF16)
    vt_ref[...] = _dot(h, wv_ref[...]).T.astype(vt_ref.dtype)

    f_logit = _dot(h, wf_ref[...]) + bf_ref[...]
    log_f = jnp.minimum(f_logit, 0.0) - jnp.log1p(jnp.exp(-jnp.abs(f_logit)))
    tri = tri_ref[...]
    c = c_carry[0:1, :] + sum(_dot(tri, piece) for piece in _split3(log_f))
    last = c[c.shape[0] - 1:, :]
    c_carry[0:1, :] = last
    edge_row = lax.broadcasted_iota(jnp.int32, edge_ref.shape, 0)
    edge_ref[...] = jnp.where(edge_row == 0, c[0:1, :], jnp.where(edge_row == 1, last, 0.0))
    c1, c2, c3 = _split3(c * LOG2_E)
    lane = lax.broadcasted_iota(jnp.int32, c.shape, 1)
    cp = jnp.where(lane < N_HEADS, c1, jnp.where(lane < 2 * N_HEADS, c2, c3))
    cp_ref[...] = cp

    k = _head_rms(_dot(h, wk_ref[...]), red_ref, bcast_ref) * kg_ref[...]
    extra = _dot(cp, place_ref[...]) + const_ref[...]
    _interleave_pairs(k, extra, kcat_ref)


def _q_kernel(x_ref, gn_ref, wq_ref, wg_ref, qg_ref, cp_ref,
              red_ref, bcast_ref, place_ref, const_ref,
              qcat_ref, gate_ref):
    xn = _rms(x_ref[...], gn_ref[...]).astype(BF16)
    gate_ref[...] = jax.nn.sigmoid(_dot(xn, wg_ref[...]))
    q = _head_rms(_dot(xn, wq_ref[...]), red_ref, bcast_ref) * qg_ref[...]
    extra = _dot(cp_ref[...], place_ref[...]) + const_ref[...]
    _interleave_pairs(q, extra, qcat_ref)


def _attn_kernel(c_first_ref, c_last_ref, floor_ref, q_ref, k_ref, vt_ref, gate_ref, o_ref,
                 qh_scr, s_scr, m_scr, acc_scr):
    b, pair, qi = (pl.program_id(a) for a in range(3))
    n_tiles = pl.num_programs(2)
    decay_floor = floor_ref[0]

    def first_needed_tile(head):
        base = (b * N_HEADS + 2 * pair + head) * n_tiles
        c_query = c_first_ref[base + qi]
        return lax.fori_loop(
            0, qi, lambda j, n: n + jnp.where(c_query - c_last_ref[base + j] < decay_floor, 1, 0), 0)

    first0, first1 = first_needed_tile(0), first_needed_tile(1)
    odd = (first0 + first1) % 2
    first0, first1 = (first0 - jnp.where((odd == 1) & (first0 > 0), 1, 0),
                      first1 - jnp.where((odd == 1) & (first0 == 0), 1, 0))
    n0, n1 = qi - first0, qi - first1
    n_full = n0 + n1

    def item(u):
        in0, in1 = u < n0, u < n_full
        return jnp.where(in0, 0, jnp.where(in1, 1, 0)), jnp.where(in0, first0 + u, jnp.where(in1, first1 + u - n0, qi))

    q = q_ref[...]
    lane_q = lax.broadcasted_iota(jnp.int32, (1, q.shape[1]), 1)
    for head in range(2):
        qh_scr[head] = jnp.where((lane_q % PAIR) // HEAD_DIM == head, q, jnp.zeros_like(q))
    ones = jnp.ones((HEAD_DIM, ATTN_TK), BF16)

    def produce(head, j, slot):
        start = pl.multiple_of(j * ATTN_TK, ATTN_TK)
        s_scr[slot] = lax.dot_general(k_ref[pl.ds(start, ATTN_TK), :], qh_scr[head], (((1,), (1,)), ((), ())),
                                      preferred_element_type=F32)

    def consume(head, j, slot, on_diagonal):
        s = s_scr[slot]
        if on_diagonal:
            key_pos = lax.broadcasted_iota(jnp.int32, s.shape, 0)
            query_pos = lax.broadcasted_iota(jnp.int32, s.shape, 1)
            s = jnp.where(key_pos <= query_pos, s, -jnp.inf)
        m = m_scr[head]
        m_new = jnp.maximum(m, jnp.max(s, axis=0, keepdims=True))
        p = jnp.exp2(s - m_new).astype(BF16)
        v_rows = pl.ds(pl.multiple_of(head * HEAD_DIM, HEAD_DIM), HEAD_DIM)
        v_aug = jnp.concatenate([vt_ref[j, v_rows, :], ones], axis=0)
        acc_scr[head] = jnp.exp2(m - m_new) * acc_scr[head] + _dot(v_aug, p)
        m_scr[head] = m_new

    def pipelined(u, n_items):
        for k in range(n_items):
            produce(*item(u + k + 1), (k + 1) % n_items)
            consume(*item(u + k), k, False)

    m_scr[...] = jnp.full(m_scr.shape, -jnp.inf, F32)
    acc_scr[...] = jnp.zeros(acc_scr.shape, F32)
    produce(*item(0), 0)

    @pl.loop(0, n_full // SCORE_SLOTS)
    def _(i):
        pipelined(SCORE_SLOTS * i, SCORE_SLOTS)

    @pl.when(n_full % SCORE_SLOTS == 2)
    def _():
        pipelined(n_full - 2, 2)

    produce(1, qi, 1)
    consume(0, qi, 0, True)
    consume(1, qi, 1, True)

    o_t = jnp.concatenate([acc_scr[head, :HEAD_DIM, :] / acc_scr[head, HEAD_DIM:, :] for head in range(2)], axis=0)
    o_ref[...] = (o_t.T * gate_ref[...]).astype(o_ref.dtype)


def _resident(shape):
    return pl.BlockSpec(shape, lambda *_: (0,) * len(shape), pipeline_mode=pl.Buffered(1))


def _row_tile(cols):
    return pl.BlockSpec((None, ROW_TILE, cols), lambda b, s: (b, s, 0))


def _dense_call(kernel, name, grid, in_specs, out_specs, out_shape, scratch_shapes=()):
    return pl.pallas_call(
        kernel, name=name, grid=grid, in_specs=in_specs, out_specs=out_specs, out_shape=out_shape,
        scratch_shapes=list(scratch_shapes),
        compiler_params=pltpu.CompilerParams(
            dimension_semantics=("arbitrary", "arbitrary"), vmem_limit_bytes=VMEM_LIMIT))


def _placement_constants():
    d = N_HEADS * HEAD_DIM
    head_of_col = jnp.arange(d) // HEAD_DIM
    pos_in_head = jnp.arange(d) % HEAD_DIM
    lane = jnp.arange(LANES)
    red = (head_of_col[:, None] == lane[None, :]).astype(F32) / HEAD_DIM
    red = jnp.concatenate([red, red], axis=0).astype(BF16)
    bcast = (lane[:, None] == head_of_col[None, :]).astype(BF16)
    bcast = jnp.concatenate([bcast, bcast], axis=0)
    piece, head = lane // N_HEADS, lane % N_HEADS
    valid = (piece < 3)[:, None] & (head[:, None] == head_of_col[None, :])
    place_q = (valid & (pos_in_head[None, :] == piece[:, None])).astype(BF16)
    place_k = -(valid & (pos_in_head[None, :] == 3 + piece[:, None])).astype(BF16)
    const_q = ((pos_in_head >= 3) & (pos_in_head < 6)).astype(F32)[None, :]
    const_k = (pos_in_head < 3).astype(F32)[None, :]
    return red, bcast, place_q, const_q, place_k, const_k


def kernel(x, attn_norm, ffn_norm, a_w_in, a_conv, a_w_out, kv_norm, w_kvf, b_f, k_norm,
           b_w_qg, q_norm, b_w_out, ffn_w_up, ffn_conv, ffn_w_down):
    bsz, s_len, d = x.shape
    depth = ffn_w_up.shape[0]
    n_a = a_w_in.shape[0]
    f = ffn_w_down.shape[1]
    assert d == N_HEADS * HEAD_DIM and s_len % ROW_TILE == 0 and s_len % ATTN_TQ == 0
    assert ATTN_TQ == ATTN_TK == ROW_TILE and d % CHUNK == 0 and f % CHUNK == 0

    grid = (bsz, s_len // ROW_TILE)
    x_shape = jax.ShapeDtypeStruct((bsz, s_len, d), F32)
    red, bcast, place_q, const_q, place_k, const_k = _placement_constants()
    row = lambda v: v.reshape(1, -1)

    def ffn_operands(l):
        return (row(ffn_norm[l]), ffn_w_up[l].astype(BF16), ffn_conv[l], ffn_w_down[l].astype(BF16))

    ffn_specs = [_resident((1, d)), _resident((d, 2 * f)), _resident((3, f)), _resident((f, d))]
    ffn_carry = pltpu.VMEM((CARRY_ROWS, f), F32)

    layer_a = _dense_call(
        _layer_a_kernel, "layer_a", grid,
        [_row_tile(d), _resident((1, d)), _resident((d, 3 * d)), _resident((3, d)), _resident((d, d))] + ffn_specs,
        _row_tile(d), x_shape, [pltpu.VMEM((CARRY_ROWS, d), F32), ffn_carry])
    for l in range(n_a):
        x = layer_a(x, row(attn_norm[l]), a_w_in[l].astype(BF16), a_conv[l], a_w_out[l].astype(BF16),
                    *ffn_operands(l))

    w_f = w_kvf[:, 2 * d:]
    pad = LANES - 3 * N_HEADS
    w_f3 = jnp.pad(jnp.concatenate([w_f, w_f, w_f], axis=1), ((0, 0), (0, pad))).astype(BF16)
    b_f3 = jnp.pad(jnp.concatenate([b_f, b_f, b_f]), (0, pad)).reshape(1, LANES)
    tri = (jnp.arange(ROW_TILE)[:, None] >= jnp.arange(ROW_TILE)[None, :]).astype(BF16)
    n_kv_tiles = s_len // ATTN_TK
    kcat, vt, cp, c_edges = _dense_call(
        _kv_kernel, "kv_proj", grid,
        [_row_tile(d), _resident((1, d)), _resident((d, d)), _resident((d, d)), _resident((d, LANES)),
         _resident((1, LANES)), _resident((1, d)), _resident((2 * d, LANES)), _resident((2 * LANES, d)),
         _resident((ROW_TILE, ROW_TILE)), _resident((LANES, d)), _resident((1, d))],
        [_row_tile(2 * d), pl.BlockSpec((None, None, d, ATTN_TK), lambda b, s: (b, s, 0, 0)), _row_tile(LANES),
         pl.BlockSpec((None, None, CARRY_ROWS, LANES), lambda b, s: (b, s, 0, 0))],
        [jax.ShapeDtypeStruct((bsz, s_len, 2 * d), BF16),
         jax.ShapeDtypeStruct((bsz, n_kv_tiles, d, ATTN_TK), BF16),
         jax.ShapeDtypeStruct((bsz, s_len, LANES), BF16),
         jax.ShapeDtypeStruct((bsz, n_kv_tiles, CARRY_ROWS, LANES), F32)],
        [pltpu.VMEM((CARRY_ROWS, LANES), F32)],
    )(x, row(kv_norm), w_kvf[:, :d].astype(BF16), w_kvf[:, d:2 * d].astype(BF16), w_f3, b_f3,
      row(jnp.tile(k_norm, N_HEADS)), red, bcast, tri, place_k, const_k)
    c_first = c_edges[:, :, 0, :N_HEADS].transpose(0, 2, 1).reshape(-1)
    c_last = c_edges[:, :, 1, :N_HEADS].transpose(0, 2, 1).reshape(-1)

    q_proj = _dense_call(
        _q_kernel, "q_proj", grid,
        [_row_tile(d), _resident((1, d)), _resident((d, d)), _resident((d, d)), _resident((1, d)),
         _row_tile(LANES), _resident((2 * d, LANES)), _resident((2 * LANES, d)),
         _resident((LANES, d)), _resident((1, d))],
        [_row_tile(2 * d), _row_tile(d)],
        [jax.ShapeDtypeStruct((bsz, s_len, 2 * d), BF16), x_shape])

    n_pairs = d // PAIR
    attention = pl.pallas_call(
        _attn_kernel, name="fox_attention",
        grid_spec=pltpu.PrefetchScalarGridSpec(
            num_scalar_prefetch=3, grid=(bsz, n_pairs, s_len // ATTN_TQ),
            in_specs=[pl.BlockSpec((None, ATTN_TQ, 2 * PAIR), lambda b, p, i, *_: (b, i, p)),
                      pl.BlockSpec((None, s_len, 2 * PAIR), lambda b, p, i, *_: (b, 0, p)),
                      pl.BlockSpec((None, n_kv_tiles, PAIR, ATTN_TK), lambda b, p, i, *_: (b, 0, p, 0)),
                      pl.BlockSpec((None, ATTN_TQ, PAIR), lambda b, p, i, *_: (b, i, p))],
            out_specs=pl.BlockSpec((None, ATTN_TQ, PAIR), lambda b, p, i, *_: (b, i, p)),
            scratch_shapes=[pltpu.VMEM((2, ATTN_TQ, 2 * PAIR), BF16),
                            pltpu.VMEM((SCORE_SLOTS, ATTN_TK, ATTN_TQ), F32),
                            pltpu.VMEM((2, 1, ATTN_TQ), F32), pltpu.VMEM((2, PAIR, ATTN_TQ), F32)]),
        out_shape=jax.ShapeDtypeStruct((bsz, s_len, d), BF16),
        compiler_params=pltpu.CompilerParams(
            dimension_semantics=("arbitrary",) * 3, vmem_limit_bytes=VMEM_LIMIT))
    k_gain = jnp.max(jnp.abs(k_norm))

    def decay_floor(q_gain):
        return (-(2.04 * HEAD_DIM ** 0.5 * jnp.max(jnp.abs(q_gain)) * k_gain + 104.0)).reshape(1).astype(F32)

    layer_b = _dense_call(
        _layer_b_kernel, "layer_b", grid,
        [_row_tile(d), _row_tile(d), _resident((d, d))] + ffn_specs,
        _row_tile(d), x_shape, [ffn_carry])

    scale = HEAD_DIM ** -0.5 * LOG2_E
    for l in range(n_a, depth):
        j = l - n_a
        qcat, gate = q_proj(x, row(attn_norm[l]), b_w_qg[j][:, :d].astype(BF16), b_w_qg[j][:, d:].astype(BF16),
                            row(jnp.tile(q_norm[j], N_HEADS)) * scale, cp, red, bcast, place_q, const_q)
        og = attention(c_first, c_last, decay_floor(q_norm[j]), qcat, kcat, vt, gate)
        x = layer_b(x, og, b_w_out[j].astype(BF16), *ffn_operands(l))
    return x
```

```python
import functools

import jax
import jax.numpy as jnp
from jax import lax
from jax.experimental import pallas as pl
from jax.experimental.pallas import tpu as pltpu

F32 = jnp.float32
BF16 = jnp.bfloat16

EPS = 1e-6
LOG2_E = 1.4426950408889634
N_HEADS = 16
HEAD_DIM = 64
LANES = 128
PAIR = 2 * HEAD_DIM
CARRY_ROWS = 8

ROW_TILE = 512
CHUNK = 256
ATTN_TQ = 512
ATTN_TK = 512
SCORE_SLOTS = 8
VMEM_LIMIT = 56 * 1024 * 1024


def _dot(a, b):
    return jnp.dot(a, b, preferred_element_type=F32)


def _rms(x, g):
    return x * lax.rsqrt(jnp.mean(x * x, axis=-1, keepdims=True) + EPS) * g


def _split2(x):
    hi = x.astype(BF16)
    lo = (x - hi.astype(F32)).astype(BF16)
    return hi, lo


def _split3(x):
    p1 = x.astype(BF16)
    r1 = x - p1.astype(F32)
    p2 = r1.astype(BF16)
    p3 = (r1 - p2.astype(F32)).astype(BF16)
    return p1, p2, p3


def _head_rms(q, red_ref, bcast_ref):
    sq_hi, sq_lo = _split2(q * q)
    ms = _dot(jnp.concatenate([sq_hi, sq_lo], axis=1), red_ref[...])
    r_hi, r_lo = _split2(lax.rsqrt(ms + EPS))
    return q * _dot(jnp.concatenate([r_hi, r_lo], axis=1), bcast_ref[...])


def _shift_down(a, first_row):
    row = lax.broadcasted_iota(jnp.int32, a.shape, 0)
    return jnp.where(row == 0, first_row, pltpu.roll(a, 1, axis=0))


def _causal_conv3(u, w, carry_ref, lo, hi):
    rows = u.shape[0]
    old = carry_ref[:, lo:hi]
    carry_ref[:, lo:hi] = u[rows - CARRY_ROWS:, :]
    u1 = _shift_down(u, old[CARRY_ROWS - 1:CARRY_ROWS, :])
    u2 = _shift_down(u1, old[CARRY_ROWS - 2:CARRY_ROWS - 1, :])
    return w[2:3, :] * u + w[1:2, :] * u1 + w[0:1, :] * u2


def _mixer(x, gn_ref, w_in_ref, cw_ref, w_out_ref, carry_ref):
    d = x.shape[1]
    xn = _rms(x, gn_ref[...]).astype(BF16)

    def project(j):
        return tuple(_dot(xn, w_in_ref[:, part * d + j * CHUNK:part * d + (j + 1) * CHUNK]) for part in range(3))

    n_chunks = d // CHUNK
    acc = jnp.zeros(x.shape, F32)
    ahead = project(0)
    for j in range(n_chunks):
        lo, hi = j * CHUNK, (j + 1) * CHUNK
        b, c, h = ahead
        if j + 1 < n_chunks:
            ahead = project(j + 1)
        u = _causal_conv3(c * h, cw_ref[:, lo:hi], carry_ref, lo, hi)
        acc = acc + _dot((b * u).astype(BF16), w_out_ref[lo:hi, :])
    return x + acc


def _conv_ffn(x, gn_ref, w_up_ref, cw_ref, w_down_ref, carry_ref):
    f = w_down_ref.shape[0]
    xn = _rms(x, gn_ref[...]).astype(BF16)

    def project(j):
        return tuple(_dot(xn, w_up_ref[:, part * f + j * CHUNK:part * f + (j + 1) * CHUNK]) for part in range(2))

    n_chunks = f // CHUNK
    acc = jnp.zeros(x.shape, F32)
    ahead = project(0)
    for j in range(n_chunks):
        lo, hi = j * CHUNK, (j + 1) * CHUNK
        a, g = ahead
        if j + 1 < n_chunks:
            ahead = project(j + 1)
        a = _causal_conv3(a, cw_ref[:, lo:hi], carry_ref, lo, hi)
        hid = a * jax.nn.sigmoid(a) * g
        acc = acc + _dot(hid.astype(BF16), w_down_ref[lo:hi, :])
    return x + acc


def _row_halves(rows):
    return [pl.ds(0, rows // 2), pl.ds(rows // 2, rows // 2)]


def _reset_at_sequence_start(*carry_refs):
    @pl.when(pl.program_id(1) == 0)
    def _():
        for ref in carry_refs:
            ref[...] = jnp.zeros(ref.shape, ref.dtype)


def _layer_a_kernel(x_ref, an_ref, w_in_ref, acw_ref, w_out_ref,
                    fn_ref, w_up_ref, fcw_ref, w_down_ref, o_ref,
                    mix_carry, ffn_carry):
    _reset_at_sequence_start(mix_carry, ffn_carry)
    halves = _row_halves(x_ref.shape[0])
    mixed = [_mixer(x_ref[rows, :], an_ref, w_in_ref, acw_ref, w_out_ref, mix_carry) for rows in halves]
    for rows, x in zip(halves, mixed):
        o_ref[rows, :] = _conv_ffn(x, fn_ref, w_up_ref, fcw_ref, w_down_ref, ffn_carry)


def _layer_b_kernel(x_ref, og_ref, w_out_ref,
                    fn_ref, w_up_ref, fcw_ref, w_down_ref, o_ref, ffn_carry):
    _reset_at_sequence_start(ffn_carry)
    halves = _row_halves(x_ref.shape[0])
    attended = [x_ref[rows, :] + _dot(og_ref[rows, :], w_out_ref[...]) for rows in halves]
    for rows, x in zip(halves, attended):
        o_ref[rows, :] = _conv_ffn(x, fn_ref, w_up_ref, fcw_ref, w_down_ref, ffn_carry)


def _interleave_pairs(main, extra, out_ref, rows):
    for p in range(main.shape[1] // PAIR):
        out_ref[rows, 2 * p * PAIR:(2 * p + 1) * PAIR] = main[:, p * PAIR:(p + 1) * PAIR].astype(out_ref.dtype)
        out_ref[rows, (2 * p + 1) * PAIR:(2 * p + 2) * PAIR] = extra[:, p * PAIR:(p + 1) * PAIR].astype(out_ref.dtype)


def _kv_kernel(x_ref, gn_ref, wk_ref, wv_ref, wf_ref, bf_ref, kg_ref,
               red_ref, bcast_ref, tri_ref, place_ref, const_ref,
               kcat_ref, vt_ref, cp_ref, edge_ref, c_carry):
    _reset_at_sequence_start(c_carry)
    halves = _row_halves(x_ref.shape[0])
    half = x_ref.shape[0] // 2
    tri = tri_ref[:half, :half]
    edges = []
    for rows in halves:
        h = _rms(x_ref[rows, :], gn_ref[...]).astype(BF16)
        vt_ref[:, rows] = _dot(h, wv_ref[...]).T.astype(vt_ref.dtype)

        f_logit = _dot(h, wf_ref[...]) + bf_ref[...]
        log_f = jnp.minimum(f_logit, 0.0) - jnp.log1p(jnp.exp(-jnp.abs(f_logit)))
        c = c_carry[0:1, :] + sum(_dot(tri, piece) for piece in _split3(log_f))
        last = c[half - 1:, :]
        c_carry[0:1, :] = last
        edges += [c[0:1, :], last]
        c1, c2, c3 = _split3(c * LOG2_E)
        lane = lax.broadcasted_iota(jnp.int32, c.shape, 1)
        cp = jnp.where(lane < N_HEADS, c1, jnp.where(lane < 2 * N_HEADS, c2, c3))
        cp_ref[rows, :] = cp

        k = _head_rms(_dot(h, wk_ref[...]), red_ref, bcast_ref) * kg_ref[...]
        extra = _dot(cp, place_ref[...]) + const_ref[...]
        _interleave_pairs(k, extra, kcat_ref, rows)
    edge_row = lax.broadcasted_iota(jnp.int32, edge_ref.shape, 0)
    edge_ref[...] = jnp.where(edge_row == 0, edges[0], jnp.where(edge_row == 1, edges[-1], 0.0))


def _q_kernel(x_ref, gn_ref, wq_ref, wg_ref, qg_ref, cp_ref,
              red_ref, bcast_ref, place_ref, const_ref,
              qcat_ref, gate_ref):
    for rows in _row_halves(x_ref.shape[0]):
        xn = _rms(x_ref[rows, :], gn_ref[...]).astype(BF16)
        gate_ref[rows, :] = jax.nn.sigmoid(_dot(xn, wg_ref[...]))
        q = _head_rms(_dot(xn, wq_ref[...]), red_ref, bcast_ref) * qg_ref[...]
        extra = _dot(cp_ref[rows, :], place_ref[...]) + const_ref[...]
        _interleave_pairs(q, extra, qcat_ref, rows)


def _attn_kernel(c_first_ref, c_last_ref, floor_ref, q_ref, k_ref, vt_ref, gate_ref, o_ref,
                 qh_scr, s_scr, m_scr, acc_scr):
    b, pair, qi = (pl.program_id(a) for a in range(3))
    n_tiles = pl.num_programs(2)
    decay_floor = floor_ref[0]

    def first_needed_tile(head):
        base = (b * N_HEADS + 2 * pair + head) * n_tiles
        c_query = c_first_ref[base + qi]
        return lax.fori_loop(
            0, qi, lambda j, n: n + jnp.where(c_query - c_last_ref[base + j] < decay_floor, 1, 0), 0)

    first0, first1 = first_needed_tile(0), first_needed_tile(1)
    odd = (first0 + first1) % 2
    first0, first1 = (first0 - jnp.where((odd == 1) & (first0 > 0), 1, 0),
                      first1 - jnp.where((odd == 1) & (first0 == 0), 1, 0))
    n0, n1 = qi - first0, qi - first1
    n_full = n0 + n1

    def item(u):
        in0, in1 = u < n0, u < n_full
        return jnp.where(in0, 0, jnp.where(in1, 1, 0)), jnp.where(in0, first0 + u, jnp.where(in1, first1 + u - n0, qi))

    q = q_ref[...]
    lane_q = lax.broadcasted_iota(jnp.int32, (1, q.shape[1]), 1)
    for head in range(2):
        qh_scr[head] = jnp.where((lane_q % PAIR) // HEAD_DIM == head, q, jnp.zeros_like(q))
    ones = jnp.ones((HEAD_DIM, ATTN_TK), BF16)

    def produce(head, j, slot):
        start = pl.multiple_of(j * ATTN_TK, ATTN_TK)
        s_scr[slot] = lax.dot_general(k_ref[pl.ds(start, ATTN_TK), :], qh_scr[head], (((1,), (1,)), ((), ())),
                                      preferred_element_type=F32)

    def consume(head, j, slot, on_diagonal):
        s = s_scr[slot]
        if on_diagonal:
            key_pos = lax.broadcasted_iota(jnp.int32, s.shape, 0)
            query_pos = lax.broadcasted_iota(jnp.int32, s.shape, 1)
            s = jnp.where(key_pos <= query_pos, s, -jnp.inf)
        m = m_scr[head]
        m_new = jnp.maximum(m, jnp.max(s, axis=0, keepdims=True))
        p = jnp.exp2(s - m_new).astype(BF16)
        v_rows = pl.ds(pl.multiple_of(head * HEAD_DIM, HEAD_DIM), HEAD_DIM)
        v_aug = jnp.concatenate([vt_ref[j, v_rows, :], ones], axis=0)
        acc_scr[head] = jnp.exp2(m - m_new) * acc_scr[head] + _dot(v_aug, p)
        m_scr[head] = m_new

    def pipelined(u, n_items):
        for k in range(n_items):
            produce(*item(u + k + 1), (k + 1) % n_items)
            consume(*item(u + k), k, False)

    m_scr[...] = jnp.full(m_scr.shape, -jnp.inf, F32)
    acc_scr[...] = jnp.zeros(acc_scr.shape, F32)
    produce(*item(0), 0)

    @pl.loop(0, n_full // SCORE_SLOTS)
    def _(i):
        pipelined(SCORE_SLOTS * i, SCORE_SLOTS)

    done = n_full - n_full % SCORE_SLOTS
    run = SCORE_SLOTS // 2
    while run >= 2:
        @pl.when(n_full & run != 0)
        def _(done=done, run=run):
            pipelined(done, run)
        done = done + (n_full & run)
        run //= 2

    produce(1, qi, 1)
    consume(0, qi, 0, True)
    consume(1, qi, 1, True)

    o_t = jnp.concatenate([acc_scr[head, :HEAD_DIM, :] / acc_scr[head, HEAD_DIM:, :] for head in range(2)], axis=0)
    o_ref[...] = (o_t.T * gate_ref[...]).astype(o_ref.dtype)


def _resident(shape):
    return pl.BlockSpec(shape, lambda *_: (0,) * len(shape), pipeline_mode=pl.Buffered(1))


def _row_tile(cols):
    return pl.BlockSpec((None, ROW_TILE, cols), lambda b, s: (b, s, 0))


def _dense_call(kernel, name, grid, in_specs, out_specs, out_shape, scratch_shapes=()):
    return pl.pallas_call(
        kernel, name=name, grid=grid, in_specs=in_specs, out_specs=out_specs, out_shape=out_shape,
        scratch_shapes=list(scratch_shapes),
        compiler_params=pltpu.CompilerParams(
            dimension_semantics=("arbitrary", "arbitrary"), vmem_limit_bytes=VMEM_LIMIT))


def _placement_constants():
    d = N_HEADS * HEAD_DIM
    head_of_col = jnp.arange(d) // HEAD_DIM
    pos_in_head = jnp.arange(d) % HEAD_DIM
    lane = jnp.arange(LANES)
    red = (head_of_col[:, None] == lane[None, :]).astype(F32) / HEAD_DIM
    red = jnp.concatenate([red, red], axis=0).astype(BF16)
    bcast = (lane[:, None] == head_of_col[None, :]).astype(BF16)
    bcast = jnp.concatenate([bcast, bcast], axis=0)
    piece, head = lane // N_HEADS, lane % N_HEADS
    valid = (piece < 3)[:, None] & (head[:, None] == head_of_col[None, :])
    place_q = (valid & (pos_in_head[None, :] == piece[:, None])).astype(BF16)
    place_k = -(valid & (pos_in_head[None, :] == 3 + piece[:, None])).astype(BF16)
    const_q = ((pos_in_head >= 3) & (pos_in_head < 6)).astype(F32)[None, :]
    const_k = (pos_in_head < 3).astype(F32)[None, :]
    return red, bcast, place_q, const_q, place_k, const_k


def kernel(x, attn_norm, ffn_norm, a_w_in, a_conv, a_w_out, kv_norm, w_kvf, b_f, k_norm,
           b_w_qg, q_norm, b_w_out, ffn_w_up, ffn_conv, ffn_w_down):
    bsz, s_len, d = x.shape
    depth = ffn_w_up.shape[0]
    n_a = a_w_in.shape[0]
    f = ffn_w_down.shape[1]
    assert d == N_HEADS * HEAD_DIM and s_len % ROW_TILE == 0 and s_len % ATTN_TQ == 0
    assert ATTN_TQ == ATTN_TK == ROW_TILE and d % CHUNK == 0 and f % CHUNK == 0

    grid = (bsz, s_len // ROW_TILE)
    x_shape = jax.ShapeDtypeStruct((bsz, s_len, d), F32)
    red, bcast, place_q, const_q, place_k, const_k = _placement_constants()
    row = lambda v: v.reshape(1, -1)

    def ffn_operands(l):
        return (row(ffn_norm[l]), ffn_w_up[l].astype(BF16), ffn_conv[l], ffn_w_down[l].astype(BF16))

    ffn_specs = [_resident((1, d)), _resident((d, 2 * f)), _resident((3, f)), _resident((f, d))]
    ffn_carry = pltpu.VMEM((CARRY_ROWS, f), F32)

    layer_a = _dense_call(
        _layer_a_kernel, "layer_a", grid,
        [_row_tile(d), _resident((1, d)), _resident((d, 3 * d)), _resident((3, d)), _resident((d, d))] + ffn_specs,
        _row_tile(d), x_shape, [pltpu.VMEM((CARRY_ROWS, d), F32), ffn_carry])
    for l in range(n_a):
        x = layer_a(x, row(attn_norm[l]), a_w_in[l].astype(BF16), a_conv[l], a_w_out[l].astype(BF16),
                    *ffn_operands(l))

    w_f = w_kvf[:, 2 * d:]
    pad = LANES - 3 * N_HEADS
    w_f3 = jnp.pad(jnp.concatenate([w_f, w_f, w_f], axis=1), ((0, 0), (0, pad))).astype(BF16)
    b_f3 = jnp.pad(jnp.concatenate([b_f, b_f, b_f]), (0, pad)).reshape(1, LANES)
    tri = (jnp.arange(ROW_TILE)[:, None] >= jnp.arange(ROW_TILE)[None, :]).astype(BF16)
    n_kv_tiles = s_len // ATTN_TK
    kcat, vt, cp, c_edges = _dense_call(
        _kv_kernel, "kv_proj", grid,
        [_row_tile(d), _resident((1, d)), _resident((d, d)), _resident((d, d)), _resident((d, LANES)),
         _resident((1, LANES)), _resident((1, d)), _resident((2 * d, LANES)), _resident((2 * LANES, d)),
         _resident((ROW_TILE, ROW_TILE)), _resident((LANES, d)), _resident((1, d))],
        [_row_tile(2 * d), pl.BlockSpec((None, None, d, ATTN_TK), lambda b, s: (b, s, 0, 0)), _row_tile(LANES),
         pl.BlockSpec((None, None, CARRY_ROWS, LANES), lambda b, s: (b, s, 0, 0))],
        [jax.ShapeDtypeStruct((bsz, s_len, 2 * d), BF16),
         jax.ShapeDtypeStruct((bsz, n_kv_tiles, d, ATTN_TK), BF16),
         jax.ShapeDtypeStruct((bsz, s_len, LANES), BF16),
         jax.ShapeDtypeStruct((bsz, n_kv_tiles, CARRY_ROWS, LANES), F32)],
        [pltpu.VMEM((CARRY_ROWS, LANES), F32)],
    )(x, row(kv_norm), w_kvf[:, :d].astype(BF16), w_kvf[:, d:2 * d].astype(BF16), w_f3, b_f3,
      row(jnp.tile(k_norm, N_HEADS)), red, bcast, tri, place_k, const_k)
    c_first = c_edges[:, :, 0, :N_HEADS].transpose(0, 2, 1).reshape(-1)
    c_last = c_edges[:, :, 1, :N_HEADS].transpose(0, 2, 1).reshape(-1)

    q_proj = _dense_call(
        _q_kernel, "q_proj", grid,
        [_row_tile(d), _resident((1, d)), _resident((d, d)), _resident((d, d)), _resident((1, d)),
         _row_tile(LANES), _resident((2 * d, LANES)), _resident((2 * LANES, d)),
         _resident((LANES, d)), _resident((1, d))],
        [_row_tile(2 * d), _row_tile(d)],
        [jax.ShapeDtypeStruct((bsz, s_len, 2 * d), BF16), x_shape])

    n_pairs = d // PAIR
    attention = pl.pallas_call(
        _attn_kernel, name="fox_attention",
        grid_spec=pltpu.PrefetchScalarGridSpec(
            num_scalar_prefetch=3, grid=(bsz, n_pairs, s_len // ATTN_TQ),
            in_specs=[pl.BlockSpec((None, ATTN_TQ, 2 * PAIR), lambda b, p, i, *_: (b, i, p)),
                      pl.BlockSpec((None, s_len, 2 * PAIR), lambda b, p, i, *_: (b, 0, p)),
                      pl.BlockSpec((None, n_kv_tiles, PAIR, ATTN_TK), lambda b, p, i, *_: (b, 0, p, 0)),
                      pl.BlockSpec((None, ATTN_TQ, PAIR), lambda b, p, i, *_: (b, i, p))],
            out_specs=pl.BlockSpec((None, ATTN_TQ, PAIR), lambda b, p, i, *_: (b, i, p)),
            scratch_shapes=[pltpu.VMEM((2, ATTN_TQ, 2 * PAIR), BF16),
                            pltpu.VMEM((SCORE_SLOTS, ATTN_TK, ATTN_TQ), F32),
                            pltpu.VMEM((2, 1, ATTN_TQ), F32), pltpu.VMEM((2, PAIR, ATTN_TQ), F32)]),
        out_shape=jax.ShapeDtypeStruct((bsz, s_len, d), BF16),
        compiler_params=pltpu.CompilerParams(
            dimension_semantics=("arbitrary",) * 3, vmem_limit_bytes=VMEM_LIMIT))
    k_gain = jnp.max(jnp.abs(k_norm))

    def decay_floor(q_gain):
        return (-(2.04 * HEAD_DIM ** 0.5 * jnp.max(jnp.abs(q_gain)) * k_gain + 104.0)).reshape(1).astype(F32)

    layer_b = _dense_call(
        _layer_b_kernel, "layer_b", grid,
        [_row_tile(d), _row_tile(d), _resident((d, d))] + ffn_specs,
        _row_tile(d), x_shape, [ffn_carry])

    scale = HEAD_DIM ** -0.5 * LOG2_E
    for l in range(n_a, depth):
        j = l - n_a
        qcat, gate = q_proj(x, row(attn_norm[l]), b_w_qg[j][:, :d].astype(BF16), b_w_qg[j][:, d:].astype(BF16),
                            row(jnp.tile(q_norm[j], N_HEADS)) * scale, cp, red, bcast, place_q, const_q)
        og = attention(c_first, c_last, decay_floor(q_norm[j]), qcat, kcat, vt, gate)
        x = layer_b(x, og, b_w_out[j].astype(BF16), *ffn_operands(l))
    return x
```

```python
import functools

import jax
import jax.numpy as jnp
from jax import lax
from jax.experimental import pallas as pl
from jax.experimental.pallas import tpu as pltpu

F32 = jnp.float32
BF16 = jnp.bfloat16

EPS = 1e-6
LOG2_E = 1.4426950408889634
N_HEADS = 16
HEAD_DIM = 64
LANES = 128
PAIR = 2 * HEAD_DIM
CARRY_ROWS = 8

ROW_TILE = 512
CHUNK = 256
ATTN_TQ = 512
ATTN_TK = 512
SCORE_SLOTS = 8
VMEM_LIMIT = 56 * 1024 * 1024


def _dot(a, b):
    return jnp.dot(a, b, preferred_element_type=F32)


def _rms(x, g):
    return x * lax.rsqrt(jnp.mean(x * x, axis=-1, keepdims=True) + EPS) * g


def _split2(x):
    hi = x.astype(BF16)
    lo = (x - hi.astype(F32)).astype(BF16)
    return hi, lo


def _split3(x):
    p1 = x.astype(BF16)
    r1 = x - p1.astype(F32)
    p2 = r1.astype(BF16)
    p3 = (r1 - p2.astype(F32)).astype(BF16)
    return p1, p2, p3


def _head_rms(q, red_ref, bcast_ref):
    sq_hi, sq_lo = _split2(q * q)
    ms = _dot(jnp.concatenate([sq_hi, sq_lo], axis=1), red_ref[...])
    r_hi, r_lo = _split2(lax.rsqrt(ms + EPS))
    return q * _dot(jnp.concatenate([r_hi, r_lo], axis=1), bcast_ref[...])


def _shift_down(a, first_row):
    row = lax.broadcasted_iota(jnp.int32, a.shape, 0)
    return jnp.where(row == 0, first_row, pltpu.roll(a, 1, axis=0))


def _causal_conv3(u, w, carry_ref, lo, hi):
    rows = u.shape[0]
    old = carry_ref[:, lo:hi]
    carry_ref[:, lo:hi] = u[rows - CARRY_ROWS:, :]
    u1 = _shift_down(u, old[CARRY_ROWS - 1:CARRY_ROWS, :])
    u2 = _shift_down(u1, old[CARRY_ROWS - 2:CARRY_ROWS - 1, :])
    return w[2:3, :] * u + w[1:2, :] * u1 + w[0:1, :] * u2


def _mixer(x, gn_ref, w_in_ref, cw_ref, w_out_ref, carry_ref):
    d = x.shape[1]
    xn = _rms(x, gn_ref[...]).astype(BF16)

    def project(j):
        return tuple(_dot(xn, w_in_ref[:, part * d + j * CHUNK:part * d + (j + 1) * CHUNK]) for part in range(3))

    n_chunks = d // CHUNK
    acc = jnp.zeros(x.shape, F32)
    ahead = project(0)
    for j in range(n_chunks):
        lo, hi = j * CHUNK, (j + 1) * CHUNK
        b, c, h = ahead
        if j + 1 < n_chunks:
            ahead = project(j + 1)
        u = _causal_conv3(c * h, cw_ref[:, lo:hi], carry_ref, lo, hi)
        acc = acc + _dot((b * u).astype(BF16), w_out_ref[lo:hi, :])
    return x + acc


def _conv_ffn(x, gn_ref, w_up_ref, cw_ref, w_down_ref, carry_ref):
    f = w_down_ref.shape[0]
    xn = _rms(x, gn_ref[...]).astype(BF16)

    def project(j):
        return tuple(_dot(xn, w_up_ref[:, part * f + j * CHUNK:part * f + (j + 1) * CHUNK]) for part in range(2))

    n_chunks = f // CHUNK
    acc = jnp.zeros(x.shape, F32)
    ahead = project(0)
    for j in range(n_chunks):
        lo, hi = j * CHUNK, (j + 1) * CHUNK
        a, g = ahead
        if j + 1 < n_chunks:
            ahead = project(j + 1)
        a = _causal_conv3(a, cw_ref[:, lo:hi], carry_ref, lo, hi)
        hid = a * jax.nn.sigmoid(a) * g
        acc = acc + _dot(hid.astype(BF16), w_down_ref[lo:hi, :])
    return x + acc


def _row_halves(rows):
    return [pl.ds(0, rows // 2), pl.ds(rows // 2, rows // 2)]


def _reset_at_sequence_start(*carry_refs):
    @pl.when(pl.program_id(1) == 0)
    def _():
        for ref in carry_refs:
            ref[...] = jnp.zeros(ref.shape, ref.dtype)


def _layer_a_kernel(x_ref, an_ref, w_in_ref, acw_ref, w_out_ref,
                    fn_ref, w_up_ref, fcw_ref, w_down_ref, o_ref,
                    mix_carry, ffn_carry):
    _reset_at_sequence_start(mix_carry, ffn_carry)
    halves = _row_halves(x_ref.shape[0])
    mixed = [_mixer(x_ref[rows, :], an_ref, w_in_ref, acw_ref, w_out_ref, mix_carry) for rows in halves]
    for rows, x in zip(halves, mixed):
        o_ref[rows, :] = _conv_ffn(x, fn_ref, w_up_ref, fcw_ref, w_down_ref, ffn_carry)


def _layer_b_kernel(x_ref, og_ref, w_out_ref,
                    fn_ref, w_up_ref, fcw_ref, w_down_ref, o_ref, ffn_carry):
    _reset_at_sequence_start(ffn_carry)
    halves = _row_halves(x_ref.shape[0])
    attended = [x_ref[rows, :] + _dot(og_ref[rows, :], w_out_ref[...]) for rows in halves]
    for rows, x in zip(halves, attended):
        o_ref[rows, :] = _conv_ffn(x, fn_ref, w_up_ref, fcw_ref, w_down_ref, ffn_carry)


def _interleave_pairs(main, extra, out_ref, rows):
    for p in range(main.shape[1] // PAIR):
        out_ref[rows, 2 * p * PAIR:(2 * p + 1) * PAIR] = main[:, p * PAIR:(p + 1) * PAIR].astype(out_ref.dtype)
        out_ref[rows, (2 * p + 1) * PAIR:(2 * p + 2) * PAIR] = extra[:, p * PAIR:(p + 1) * PAIR].astype(out_ref.dtype)


def _kv_kernel(x_ref, gn_ref, wk_ref, wv_ref, wf_ref, bf_ref, kg_ref,
               red_ref, bcast_ref, tri_ref, place_ref, const_ref,
               kcat_ref, vt_ref, cp_ref, edge_ref, c_carry):
    _reset_at_sequence_start(c_carry)
    halves = _row_halves(x_ref.shape[0])
    half = x_ref.shape[0] // 2
    tri = tri_ref[:half, :half]
    edges = []
    for rows in halves:
        h = _rms(x_ref[rows, :], gn_ref[...]).astype(BF16)
        vt_ref[:, rows] = _dot(h, wv_ref[...]).T.astype(vt_ref.dtype)

        f_logit = _dot(h, wf_ref[...]) + bf_ref[...]
        log_f = jnp.minimum(f_logit, 0.0) - jnp.log1p(jnp.exp(-jnp.abs(f_logit)))
        c = c_carry[0:1, :] + sum(_dot(tri, piece) for piece in _split3(log_f))
        last = c[half - 1:, :]
        c_carry[0:1, :] = last
        edges += [c[0:1, :], last]
        c1, c2, c3 = _split3(c * LOG2_E)
        lane = lax.broadcasted_iota(jnp.int32, c.shape, 1)
        cp = jnp.where(lane < N_HEADS, c1, jnp.where(lane < 2 * N_HEADS, c2, c3))
        cp_ref[rows, :] = cp

        k = _head_rms(_dot(h, wk_ref[...]), red_ref, bcast_ref) * kg_ref[...]
        extra = _dot(cp, place_ref[...]) + const_ref[...]
        _interleave_pairs(k, extra, kcat_ref, rows)
    edge_row = lax.broadcasted_iota(jnp.int32, edge_ref.shape, 0)
    edge_ref[...] = jnp.where(edge_row == 0, edges[0], jnp.where(edge_row == 1, edges[-1], 0.0))


def _q_kernel(x_ref, gn_ref, wq_ref, wg_ref, qg_ref, cp_ref,
              red_ref, bcast_ref, place_ref, const_ref,
              qcat_ref, gate_ref):
    for rows in _row_halves(x_ref.shape[0]):
        xn = _rms(x_ref[rows, :], gn_ref[...]).astype(BF16)
        gate_ref[rows, :] = jax.nn.sigmoid(_dot(xn, wg_ref[...]))
        q = _head_rms(_dot(xn, wq_ref[...]), red_ref, bcast_ref) * qg_ref[...]
        extra = _dot(cp_ref[rows, :], place_ref[...]) + const_ref[...]
        _interleave_pairs(q, extra, qcat_ref, rows)


def _attn_kernel(c_first_ref, c_last_ref, floor_ref, q_ref, k_ref, vt_ref, gate_ref, o_ref,
                 qh_scr, s_scr, m_scr, acc_scr):
    n_tiles = q_ref.shape[0] // ATTN_TQ
    pl.loop(0, n_tiles)(functools.partial(
        _attend_query_tile, c_first_ref, c_last_ref, floor_ref, q_ref, k_ref, vt_ref, gate_ref, o_ref,
        qh_scr, s_scr, m_scr, acc_scr, n_tiles))


def _attend_query_tile(c_first_ref, c_last_ref, floor_ref, q_ref, k_ref, vt_ref, gate_ref, o_ref,
                       qh_scr, s_scr, m_scr, acc_scr, n_tiles, qi):
    b, pair = pl.program_id(0), pl.program_id(1)
    decay_floor = floor_ref[0]
    q_rows = pl.ds(pl.multiple_of(qi * ATTN_TQ, ATTN_TQ), ATTN_TQ)

    def first_needed_tile(head):
        base = (b * N_HEADS + 2 * pair + head) * n_tiles
        c_query = c_first_ref[base + qi]
        return lax.fori_loop(
            0, qi, lambda j, n: n + jnp.where(c_query - c_last_ref[base + j] < decay_floor, 1, 0), 0)

    first0, first1 = first_needed_tile(0), first_needed_tile(1)
    odd = (first0 + first1) % 2
    first0, first1 = (first0 - jnp.where((odd == 1) & (first0 > 0), 1, 0),
                      first1 - jnp.where((odd == 1) & (first0 == 0), 1, 0))
    n0, n1 = qi - first0, qi - first1
    n_full = n0 + n1

    def item(u):
        in0, in1 = u < n0, u < n_full
        return jnp.where(in0, 0, jnp.where(in1, 1, 0)), jnp.where(in0, first0 + u, jnp.where(in1, first1 + u - n0, qi))

    q = q_ref[q_rows, :]
    lane_q = lax.broadcasted_iota(jnp.int32, (1, q.shape[1]), 1)
    for head in range(2):
        qh_scr[head] = jnp.where((lane_q % PAIR) // HEAD_DIM == head, q, jnp.zeros_like(q))
    ones = jnp.ones((HEAD_DIM, ATTN_TK), BF16)

    def produce(head, j, slot):
        start = pl.multiple_of(j * ATTN_TK, ATTN_TK)
        s_scr[slot] = lax.dot_general(k_ref[pl.ds(start, ATTN_TK), :], qh_scr[head], (((1,), (1,)), ((), ())),
                                      preferred_element_type=F32)

    def consume(head, j, slot, on_diagonal):
        s = s_scr[slot]
        if on_diagonal:
            key_pos = lax.broadcasted_iota(jnp.int32, s.shape, 0)
            query_pos = lax.broadcasted_iota(jnp.int32, s.shape, 1)
            s = jnp.where(key_pos <= query_pos, s, -jnp.inf)
        m = m_scr[head]
        m_new = jnp.maximum(m, jnp.max(s, axis=0, keepdims=True))
        p = jnp.exp2(s - m_new).astype(BF16)
        v_rows = pl.ds(pl.multiple_of(head * HEAD_DIM, HEAD_DIM), HEAD_DIM)
        v_aug = jnp.concatenate([vt_ref[j, v_rows, :], ones], axis=0)
        acc_scr[head] = jnp.exp2(m - m_new) * acc_scr[head] + _dot(v_aug, p)
        m_scr[head] = m_new

    def pipelined(u, n_items):
        for k in range(n_items):
            produce(*item(u + k + 1), (k + 1) % n_items)
            consume(*item(u + k), k, False)

    m_scr[...] = jnp.full(m_scr.shape, -jnp.inf, F32)
    acc_scr[...] = jnp.zeros(acc_scr.shape, F32)
    produce(*item(0), 0)

    @pl.loop(0, n_full // SCORE_SLOTS)
    def _(i):
        pipelined(SCORE_SLOTS * i, SCORE_SLOTS)

    done = n_full - n_full % SCORE_SLOTS
    run = SCORE_SLOTS // 2
    while run >= 2:
        @pl.when(n_full & run != 0)
        def _(done=done, run=run):
            pipelined(done, run)
        done = done + (n_full & run)
        run //= 2

    produce(1, qi, 1)
    consume(0, qi, 0, True)
    consume(1, qi, 1, True)

    o_t = jnp.concatenate([acc_scr[head, :HEAD_DIM, :] / acc_scr[head, HEAD_DIM:, :] for head in range(2)], axis=0)
    o_ref[q_rows, :] = (o_t.T * gate_ref[q_rows, :]).astype(o_ref.dtype)


def _resident(shape):
    return pl.BlockSpec(shape, lambda *_: (0,) * len(shape), pipeline_mode=pl.Buffered(1))


def _row_tile(cols):
    return pl.BlockSpec((None, ROW_TILE, cols), lambda b, s: (b, s, 0))


def _dense_call(kernel, name, grid, in_specs, out_specs, out_shape, scratch_shapes=()):
    return pl.pallas_call(
        kernel, name=name, grid=grid, in_specs=in_specs, out_specs=out_specs, out_shape=out_shape,
        scratch_shapes=list(scratch_shapes),
        compiler_params=pltpu.CompilerParams(
            dimension_semantics=("arbitrary", "arbitrary"), vmem_limit_bytes=VMEM_LIMIT))


def _placement_constants():
    d = N_HEADS * HEAD_DIM
    head_of_col = jnp.arange(d) // HEAD_DIM
    pos_in_head = jnp.arange(d) % HEAD_DIM
    lane = jnp.arange(LANES)
    red = (head_of_col[:, None] == lane[None, :]).astype(F32) / HEAD_DIM
    red = jnp.concatenate([red, red], axis=0).astype(BF16)
    bcast = (lane[:, None] == head_of_col[None, :]).astype(BF16)
    bcast = jnp.concatenate([bcast, bcast], axis=0)
    piece, head = lane // N_HEADS, lane % N_HEADS
    valid = (piece < 3)[:, None] & (head[:, None] == head_of_col[None, :])
    place_q = (valid & (pos_in_head[None, :] == piece[:, None])).astype(BF16)
    place_k = -(valid & (pos_in_head[None, :] == 3 + piece[:, None])).astype(BF16)
    const_q = ((pos_in_head >= 3) & (pos_in_head < 6)).astype(F32)[None, :]
    const_k = (pos_in_head < 3).astype(F32)[None, :]
    return red, bcast, place_q, const_q, place_k, const_k


def kernel(x, attn_norm, ffn_norm, a_w_in, a_conv, a_w_out, kv_norm, w_kvf, b_f, k_norm,
           b_w_qg, q_norm, b_w_out, ffn_w_up, ffn_conv, ffn_w_down):
    bsz, s_len, d = x.shape
    depth = ffn_w_up.shape[0]
    n_a = a_w_in.shape[0]
    f = ffn_w_down.shape[1]
    assert d == N_HEADS * HEAD_DIM and s_len % ROW_TILE == 0 and s_len % ATTN_TQ == 0
    assert ATTN_TQ == ATTN_TK == ROW_TILE and d % CHUNK == 0 and f % CHUNK == 0

    grid = (bsz, s_len // ROW_TILE)
    x_shape = jax.ShapeDtypeStruct((bsz, s_len, d), F32)
    red, bcast, place_q, const_q, place_k, const_k = _placement_constants()
    row = lambda v: v.reshape(1, -1)

    def ffn_operands(l):
        return (row(ffn_norm[l]), ffn_w_up[l].astype(BF16), ffn_conv[l], ffn_w_down[l].astype(BF16))

    ffn_specs = [_resident((1, d)), _resident((d, 2 * f)), _resident((3, f)), _resident((f, d))]
    ffn_carry = pltpu.VMEM((CARRY_ROWS, f), F32)

    layer_a = _dense_call(
        _layer_a_kernel, "layer_a", grid,
        [_row_tile(d), _resident((1, d)), _resident((d, 3 * d)), _resident((3, d)), _resident((d, d))] + ffn_specs,
        _row_tile(d), x_shape, [pltpu.VMEM((CARRY_ROWS, d), F32), ffn_carry])
    for l in range(n_a):
        x = layer_a(x, row(attn_norm[l]), a_w_in[l].astype(BF16), a_conv[l], a_w_out[l].astype(BF16),
                    *ffn_operands(l))

    w_f = w_kvf[:, 2 * d:]
    pad = LANES - 3 * N_HEADS
    w_f3 = jnp.pad(jnp.concatenate([w_f, w_f, w_f], axis=1), ((0, 0), (0, pad))).astype(BF16)
    b_f3 = jnp.pad(jnp.concatenate([b_f, b_f, b_f]), (0, pad)).reshape(1, LANES)
    tri = (jnp.arange(ROW_TILE)[:, None] >= jnp.arange(ROW_TILE)[None, :]).astype(BF16)
    n_kv_tiles = s_len // ATTN_TK
    kcat, vt, cp, c_edges = _dense_call(
        _kv_kernel, "kv_proj", grid,
        [_row_tile(d), _resident((1, d)), _resident((d, d)), _resident((d, d)), _resident((d, LANES)),
         _resident((1, LANES)), _resident((1, d)), _resident((2 * d, LANES)), _resident((2 * LANES, d)),
         _resident((ROW_TILE, ROW_TILE)), _resident((LANES, d)), _resident((1, d))],
        [_row_tile(2 * d), pl.BlockSpec((None, None, d, ATTN_TK), lambda b, s: (b, s, 0, 0)), _row_tile(LANES),
         pl.BlockSpec((None, None, CARRY_ROWS, LANES), lambda b, s: (b, s, 0, 0))],
        [jax.ShapeDtypeStruct((bsz, s_len, 2 * d), BF16),
         jax.ShapeDtypeStruct((bsz, n_kv_tiles, d, ATTN_TK), BF16),
         jax.ShapeDtypeStruct((bsz, s_len, LANES), BF16),
         jax.ShapeDtypeStruct((bsz, n_kv_tiles, CARRY_ROWS, LANES), F32)],
        [pltpu.VMEM((CARRY_ROWS, LANES), F32)],
    )(x, row(kv_norm), w_kvf[:, :d].astype(BF16), w_kvf[:, d:2 * d].astype(BF16), w_f3, b_f3,
      row(jnp.tile(k_norm, N_HEADS)), red, bcast, tri, place_k, const_k)
    c_first = c_edges[:, :, 0, :N_HEADS].transpose(0, 2, 1).reshape(-1)
    c_last = c_edges[:, :, 1, :N_HEADS].transpose(0, 2, 1).reshape(-1)

    q_proj = _dense_call(
        _q_kernel, "q_proj", grid,
        [_row_tile(d), _resident((1, d)), _resident((d, d)), _resident((d, d)), _resident((1, d)),
         _row_tile(LANES), _resident((2 * d, LANES)), _resident((2 * LANES, d)),
         _resident((LANES, d)), _resident((1, d))],
        [_row_tile(2 * d), _row_tile(d)],
        [jax.ShapeDtypeStruct((bsz, s_len, 2 * d), BF16), x_shape])

    n_pairs = d // PAIR
    attention = pl.pallas_call(
        _attn_kernel, name="fox_attention",
        grid_spec=pltpu.PrefetchScalarGridSpec(
            num_scalar_prefetch=3, grid=(bsz, n_pairs),
            in_specs=[pl.BlockSpec((None, s_len, 2 * PAIR), lambda b, p, *_: (b, 0, p)),
                      pl.BlockSpec((None, s_len, 2 * PAIR), lambda b, p, *_: (b, 0, p)),
                      pl.BlockSpec((None, n_kv_tiles, PAIR, ATTN_TK), lambda b, p, *_: (b, 0, p, 0)),
                      pl.BlockSpec((None, s_len, PAIR), lambda b, p, *_: (b, 0, p))],
            out_specs=pl.BlockSpec((None, s_len, PAIR), lambda b, p, *_: (b, 0, p)),
            scratch_shapes=[pltpu.VMEM((2, ATTN_TQ, 2 * PAIR), BF16),
                            pltpu.VMEM((SCORE_SLOTS, ATTN_TK, ATTN_TQ), F32),
                            pltpu.VMEM((2, 1, ATTN_TQ), F32), pltpu.VMEM((2, PAIR, ATTN_TQ), F32)]),
        out_shape=jax.ShapeDtypeStruct((bsz, s_len, d), BF16),
        compiler_params=pltpu.CompilerParams(
            dimension_semantics=("arbitrary",) * 2, vmem_limit_bytes=VMEM_LIMIT))
    k_gain = jnp.max(jnp.abs(k_norm))

    def decay_floor(q_gain):
        return (-(2.04 * HEAD_DIM ** 0.5 * jnp.max(jnp.abs(q_gain)) * k_gain + 104.0)).reshape(1).astype(F32)

    layer_b = _dense_call(
        _layer_b_kernel, "layer_b", grid,
        [_row_tile(d), _row_tile(d), _resident((d, d))] + ffn_specs,
        _row_tile(d), x_shape, [ffn_carry])

    scale = HEAD_DIM ** -0.5 * LOG2_E
    for l in range(n_a, depth):
        j = l - n_a
        qcat, gate = q_proj(x, row(attn_norm[l]), b_w_qg[j][:, :d].astype(BF16), b_w_qg[j][:, d:].astype(BF16),
                            row(jnp.tile(q_norm[j], N_HEADS)) * scale, cp, red, bcast, place_q, const_q)
        og = attention(c_first, c_last, decay_floor(q_norm[j]), qcat, kcat, vt, gate)
        x = layer_b(x, og, b_w_out[j].astype(BF16), *ffn_operands(l))
    return x
```

```python
import functools

import jax
import jax.numpy as jnp
from jax import lax
from jax.experimental import pallas as pl
from jax.experimental.pallas import tpu as pltpu

F32 = jnp.float32
BF16 = jnp.bfloat16

EPS = 1e-6
LOG2_E = 1.4426950408889634
N_HEADS = 16
HEAD_DIM = 64
LANES = 128
PAIR = 2 * HEAD_DIM
CARRY_ROWS = 8

ROW_TILE = 512
CHUNK = 256
ATTN_TQ = 512
ATTN_TK = 512
SCORE_SLOTS = 8
NEGLIGIBLE_LOG_WEIGHT = 88.0
VMEM_LIMIT = 56 * 1024 * 1024


def _dot(a, b):
    return jnp.dot(a, b, preferred_element_type=F32)


def _rms(x, g):
    return x * lax.rsqrt(jnp.mean(x * x, axis=-1, keepdims=True) + EPS) * g


def _split2(x):
    hi = x.astype(BF16)
    lo = (x - hi.astype(F32)).astype(BF16)
    return hi, lo


def _split3(x):
    p1 = x.astype(BF16)
    r1 = x - p1.astype(F32)
    p2 = r1.astype(BF16)
    p3 = (r1 - p2.astype(F32)).astype(BF16)
    return p1, p2, p3


def _head_rms(q, red_ref, bcast_ref):
    sq_hi, sq_lo = _split2(q * q)
    ms = _dot(jnp.concatenate([sq_hi, sq_lo], axis=1), red_ref[...])
    r_hi, r_lo = _split2(lax.rsqrt(ms + EPS))
    return q * _dot(jnp.concatenate([r_hi, r_lo], axis=1), bcast_ref[...])


def _shift_down(a, first_row):
    row = lax.broadcasted_iota(jnp.int32, a.shape, 0)
    return jnp.where(row == 0, first_row, pltpu.roll(a, 1, axis=0))


def _causal_conv3(u, w, carry_ref, lo, hi):
    rows = u.shape[0]
    old = carry_ref[:, lo:hi]
    carry_ref[:, lo:hi] = u[rows - CARRY_ROWS:, :]
    u1 = _shift_down(u, old[CARRY_ROWS - 1:CARRY_ROWS, :])
    u2 = _shift_down(u1, old[CARRY_ROWS - 2:CARRY_ROWS - 1, :])
    return w[2:3, :] * u + w[1:2, :] * u1 + w[0:1, :] * u2


def _mixer(x, gn_ref, w_in_ref, cw_ref, w_out_ref, carry_ref):
    d = x.shape[1]
    xn = _rms(x, gn_ref[...]).astype(BF16)

    def project(j):
        return tuple(_dot(xn, w_in_ref[:, part * d + j * CHUNK:part * d + (j + 1) * CHUNK]) for part in range(3))

    n_chunks = d // CHUNK
    acc = jnp.zeros(x.shape, F32)
    ahead = project(0)
    for j in range(n_chunks):
        lo, hi = j * CHUNK, (j + 1) * CHUNK
        b, c, h = ahead
        if j + 1 < n_chunks:
            ahead = project(j + 1)
        u = _causal_conv3(c * h, cw_ref[:, lo:hi], carry_ref, lo, hi)
        acc = acc + _dot((b * u).astype(BF16), w_out_ref[lo:hi, :])
    return x + acc


def _conv_ffn(x, gn_ref, w_up_ref, cw_ref, w_down_ref, carry_ref):
    f = w_down_ref.shape[0]
    xn = _rms(x, gn_ref[...]).astype(BF16)

    def project(j):
        return tuple(_dot(xn, w_up_ref[:, part * f + j * CHUNK:part * f + (j + 1) * CHUNK]) for part in range(2))

    n_chunks = f // CHUNK
    acc = jnp.zeros(x.shape, F32)
    ahead = project(0)
    for j in range(n_chunks):
        lo, hi = j * CHUNK, (j + 1) * CHUNK
        a, g = ahead
        if j + 1 < n_chunks:
            ahead = project(j + 1)
        a = _causal_conv3(a, cw_ref[:, lo:hi], carry_ref, lo, hi)
        hid = a * jax.nn.sigmoid(a) * g
        acc = acc + _dot(hid.astype(BF16), w_down_ref[lo:hi, :])
    return x + acc


def _row_halves(rows):
    return [pl.ds(0, rows // 2), pl.ds(rows // 2, rows // 2)]


def _reset_at_sequence_start(*carry_refs):
    @pl.when(pl.program_id(1) == 0)
    def _():
        for ref in carry_refs:
            ref[...] = jnp.zeros(ref.shape, ref.dtype)


def _layer_a_kernel(x_ref, an_ref, w_in_ref, acw_ref, w_out_ref,
                    fn_ref, w_up_ref, fcw_ref, w_down_ref, o_ref,
                    mix_carry, ffn_carry):
    _reset_at_sequence_start(mix_carry, ffn_carry)
    halves = _row_halves(x_ref.shape[0])
    mixed = [_mixer(x_ref[rows, :], an_ref, w_in_ref, acw_ref, w_out_ref, mix_carry) for rows in halves]
    for rows, x in zip(halves, mixed):
        o_ref[rows, :] = _conv_ffn(x, fn_ref, w_up_ref, fcw_ref, w_down_ref, ffn_carry)


def _layer_b_kernel(x_ref, og_ref, w_out_ref,
                    fn_ref, w_up_ref, fcw_ref, w_down_ref, o_ref, ffn_carry):
    _reset_at_sequence_start(ffn_carry)
    halves = _row_halves(x_ref.shape[0])
    attended = [x_ref[rows, :] + _dot(og_ref[rows, :], w_out_ref[...]) for rows in halves]
    for rows, x in zip(halves, attended):
        o_ref[rows, :] = _conv_ffn(x, fn_ref, w_up_ref, fcw_ref, w_down_ref, ffn_carry)


def _interleave_pairs(main, extra, out_ref, rows):
    for p in range(main.shape[1] // PAIR):
        out_ref[rows, 2 * p * PAIR:(2 * p + 1) * PAIR] = main[:, p * PAIR:(p + 1) * PAIR].astype(out_ref.dtype)
        out_ref[rows, (2 * p + 1) * PAIR:(2 * p + 2) * PAIR] = extra[:, p * PAIR:(p + 1) * PAIR].astype(out_ref.dtype)


def _kv_kernel(x_ref, gn_ref, wk_ref, wv_ref, wf_ref, bf_ref, kg_ref,
               red_ref, bcast_ref, tri_ref, place_ref, const_ref,
               kcat_ref, vt_ref, cp_ref, edge_ref, c_carry):
    _reset_at_sequence_start(c_carry)
    halves = _row_halves(x_ref.shape[0])
    half = x_ref.shape[0] // 2
    tri = tri_ref[:half, :half]
    edges = []
    for rows in halves:
        h = _rms(x_ref[rows, :], gn_ref[...]).astype(BF16)
        vt_ref[:, rows] = _dot(h, wv_ref[...]).T.astype(vt_ref.dtype)

        f_logit = _dot(h, wf_ref[...]) + bf_ref[...]
        log_f = jnp.minimum(f_logit, 0.0) - jnp.log1p(jnp.exp(-jnp.abs(f_logit)))
        c = c_carry[0:1, :] + sum(_dot(tri, piece) for piece in _split3(log_f))
        last = c[half - 1:, :]
        c_carry[0:1, :] = last
        edges += [c[0:1, :], last]
        c1, c2, c3 = _split3(c * LOG2_E)
        lane = lax.broadcasted_iota(jnp.int32, c.shape, 1)
        cp = jnp.where(lane < N_HEADS, c1, jnp.where(lane < 2 * N_HEADS, c2, c3))
        cp_ref[rows, :] = cp

        k = _head_rms(_dot(h, wk_ref[...]), red_ref, bcast_ref) * kg_ref[...]
        extra = _dot(cp, place_ref[...]) + const_ref[...]
        _interleave_pairs(k, extra, kcat_ref, rows)
    edge_row = lax.broadcasted_iota(jnp.int32, edge_ref.shape, 0)
    edge_ref[...] = jnp.where(edge_row == 0, edges[0], jnp.where(edge_row == 1, edges[-1], 0.0))


def _q_kernel(x_ref, gn_ref, wq_ref, wg_ref, qg_ref, cp_ref,
              red_ref, bcast_ref, place_ref, const_ref,
              qcat_ref, gate_ref):
    for rows in _row_halves(x_ref.shape[0]):
        xn = _rms(x_ref[rows, :], gn_ref[...]).astype(BF16)
        gate_ref[rows, :] = jax.nn.sigmoid(_dot(xn, wg_ref[...]))
        q = _head_rms(_dot(xn, wq_ref[...]), red_ref, bcast_ref) * qg_ref[...]
        extra = _dot(cp_ref[rows, :], place_ref[...]) + const_ref[...]
        _interleave_pairs(q, extra, qcat_ref, rows)


def _attn_kernel(c_first_ref, c_last_ref, floor_ref, q_ref, k_ref, vt_ref, gate_ref, o_ref,
                 qh_scr, s_scr, m_scr, acc_scr):
    n_tiles = q_ref.shape[0] // ATTN_TQ
    pl.loop(0, n_tiles)(functools.partial(
        _attend_query_tile, c_first_ref, c_last_ref, floor_ref, q_ref, k_ref, vt_ref, gate_ref, o_ref,
        qh_scr, s_scr, m_scr, acc_scr, n_tiles))


def _attend_query_tile(c_first_ref, c_last_ref, floor_ref, q_ref, k_ref, vt_ref, gate_ref, o_ref,
                       qh_scr, s_scr, m_scr, acc_scr, n_tiles, qi):
    b, pair = pl.program_id(0), pl.program_id(1)
    decay_floor = floor_ref[0]
    q_rows = pl.ds(pl.multiple_of(qi * ATTN_TQ, ATTN_TQ), ATTN_TQ)

    def first_needed_tile(head):
        base = (b * N_HEADS + 2 * pair + head) * n_tiles
        c_query = c_first_ref[base + qi]
        return lax.fori_loop(
            0, qi, lambda j, n: n + jnp.where(c_query - c_last_ref[base + j] < decay_floor, 1, 0), 0)

    first0, first1 = first_needed_tile(0), first_needed_tile(1)
    odd = (first0 + first1) % 2
    first0, first1 = (first0 - jnp.where((odd == 1) & (first0 > 0), 1, 0),
                      first1 - jnp.where((odd == 1) & (first0 == 0), 1, 0))
    n0, n1 = qi - first0, qi - first1
    n_full = n0 + n1

    def item(u):
        in0, in1 = u < n0, u < n_full
        return jnp.where(in0, 0, jnp.where(in1, 1, 0)), jnp.where(in0, first0 + u, jnp.where(in1, first1 + u - n0, qi))

    q = q_ref[q_rows, :]
    lane_q = lax.broadcasted_iota(jnp.int32, (1, q.shape[1]), 1)
    for head in range(2):
        qh_scr[head] = jnp.where((lane_q % PAIR) // HEAD_DIM == head, q, jnp.zeros_like(q))
    ones = jnp.ones((HEAD_DIM, ATTN_TK), BF16)

    def produce(head, j, slot):
        start = pl.multiple_of(j * ATTN_TK, ATTN_TK)
        s_scr[slot] = lax.dot_general(k_ref[pl.ds(start, ATTN_TK), :], qh_scr[head], (((1,), (1,)), ((), ())),
                                      preferred_element_type=F32)

    def consume(head, j, slot, on_diagonal):
        s = s_scr[slot]
        if on_diagonal:
            key_pos = lax.broadcasted_iota(jnp.int32, s.shape, 0)
            query_pos = lax.broadcasted_iota(jnp.int32, s.shape, 1)
            s = jnp.where(key_pos <= query_pos, s, -jnp.inf)
        m = m_scr[head]
        m_new = jnp.maximum(m, jnp.max(s, axis=0, keepdims=True))
        p = jnp.exp2(s - m_new).astype(BF16)
        v_rows = pl.ds(pl.multiple_of(head * HEAD_DIM, HEAD_DIM), HEAD_DIM)
        v_aug = jnp.concatenate([vt_ref[j, v_rows, :], ones], axis=0)
        acc_scr[head] = jnp.exp2(m - m_new) * acc_scr[head] + _dot(v_aug, p)
        m_scr[head] = m_new

    def pipelined(u, n_items):
        for k in range(n_items):
            produce(*item(u + k + 1), (k + 1) % n_items)
            consume(*item(u + k), k, False)

    m_scr[...] = jnp.full(m_scr.shape, -jnp.inf, F32)
    acc_scr[...] = jnp.zeros(acc_scr.shape, F32)
    produce(*item(0), 0)

    @pl.loop(0, n_full // SCORE_SLOTS)
    def _(i):
        pipelined(SCORE_SLOTS * i, SCORE_SLOTS)

    done = n_full - n_full % SCORE_SLOTS
    run = SCORE_SLOTS // 2
    while run >= 2:
        @pl.when(n_full & run != 0)
        def _(done=done, run=run):
            pipelined(done, run)
        done = done + (n_full & run)
        run //= 2

    produce(1, qi, 1)
    consume(0, qi, 0, True)
    consume(1, qi, 1, True)

    o_t = jnp.concatenate([acc_scr[head, :HEAD_DIM, :] / acc_scr[head, HEAD_DIM:, :] for head in range(2)], axis=0)
    o_ref[q_rows, :] = (o_t.T * gate_ref[q_rows, :]).astype(o_ref.dtype)


def _resident(shape):
    return pl.BlockSpec(shape, lambda *_: (0,) * len(shape), pipeline_mode=pl.Buffered(1))


def _row_tile(cols):
    return pl.BlockSpec((None, ROW_TILE, cols), lambda b, s: (b, s, 0))


def _dense_call(kernel, name, grid, in_specs, out_specs, out_shape, scratch_shapes=()):
    return pl.pallas_call(
        kernel, name=name, grid=grid, in_specs=in_specs, out_specs=out_specs, out_shape=out_shape,
        scratch_shapes=list(scratch_shapes),
        compiler_params=pltpu.CompilerParams(
            dimension_semantics=("arbitrary", "arbitrary"), vmem_limit_bytes=VMEM_LIMIT))


def _placement_constants():
    d = N_HEADS * HEAD_DIM
    head_of_col = jnp.arange(d) // HEAD_DIM
    pos_in_head = jnp.arange(d) % HEAD_DIM
    lane = jnp.arange(LANES)
    red = (head_of_col[:, None] == lane[None, :]).astype(F32) / HEAD_DIM
    red = jnp.concatenate([red, red], axis=0).astype(BF16)
    bcast = (lane[:, None] == head_of_col[None, :]).astype(BF16)
    bcast = jnp.concatenate([bcast, bcast], axis=0)
    piece, head = lane // N_HEADS, lane % N_HEADS
    valid = (piece < 3)[:, None] & (head[:, None] == head_of_col[None, :])
    place_q = (valid & (pos_in_head[None, :] == piece[:, None])).astype(BF16)
    place_k = -(valid & (pos_in_head[None, :] == 3 + piece[:, None])).astype(BF16)
    const_q = ((pos_in_head >= 3) & (pos_in_head < 6)).astype(F32)[None, :]
    const_k = (pos_in_head < 3).astype(F32)[None, :]
    return red, bcast, place_q, const_q, place_k, const_k


def kernel(x, attn_norm, ffn_norm, a_w_in, a_conv, a_w_out, kv_norm, w_kvf, b_f, k_norm,
           b_w_qg, q_norm, b_w_out, ffn_w_up, ffn_conv, ffn_w_down):
    bsz, s_len, d = x.shape
    depth = ffn_w_up.shape[0]
    n_a = a_w_in.shape[0]
    f = ffn_w_down.shape[1]
    assert d == N_HEADS * HEAD_DIM and s_len % ROW_TILE == 0 and s_len % ATTN_TQ == 0
    assert ATTN_TQ == ATTN_TK == ROW_TILE and d % CHUNK == 0 and f % CHUNK == 0

    grid = (bsz, s_len // ROW_TILE)
    x_shape = jax.ShapeDtypeStruct((bsz, s_len, d), F32)
    red, bcast, place_q, const_q, place_k, const_k = _placement_constants()
    row = lambda v: v.reshape(1, -1)

    def ffn_operands(l):
        return (row(ffn_norm[l]), ffn_w_up[l].astype(BF16), ffn_conv[l], ffn_w_down[l].astype(BF16))

    ffn_specs = [_resident((1, d)), _resident((d, 2 * f)), _resident((3, f)), _resident((f, d))]
    ffn_carry = pltpu.VMEM((CARRY_ROWS, f), F32)

    layer_a = _dense_call(
        _layer_a_kernel, "layer_a", grid,
        [_row_tile(d), _resident((1, d)), _resident((d, 3 * d)), _resident((3, d)), _resident((d, d))] + ffn_specs,
        _row_tile(d), x_shape, [pltpu.VMEM((CARRY_ROWS, d), F32), ffn_carry])
    for l in range(n_a):
        x = layer_a(x, row(attn_norm[l]), a_w_in[l].astype(BF16), a_conv[l], a_w_out[l].astype(BF16),
                    *ffn_operands(l))

    w_f = w_kvf[:, 2 * d:]
    pad = LANES - 3 * N_HEADS
    w_f3 = jnp.pad(jnp.concatenate([w_f, w_f, w_f], axis=1), ((0, 0), (0, pad))).astype(BF16)
    b_f3 = jnp.pad(jnp.concatenate([b_f, b_f, b_f]), (0, pad)).reshape(1, LANES)
    tri = (jnp.arange(ROW_TILE)[:, None] >= jnp.arange(ROW_TILE)[None, :]).astype(BF16)
    n_kv_tiles = s_len // ATTN_TK
    kcat, vt, cp, c_edges = _dense_call(
        _kv_kernel, "kv_proj", grid,
        [_row_tile(d), _resident((1, d)), _resident((d, d)), _resident((d, d)), _resident((d, LANES)),
         _resident((1, LANES)), _resident((1, d)), _resident((2 * d, LANES)), _resident((2 * LANES, d)),
         _resident((ROW_TILE, ROW_TILE)), _resident((LANES, d)), _resident((1, d))],
        [_row_tile(2 * d), pl.BlockSpec((None, None, d, ATTN_TK), lambda b, s: (b, s, 0, 0)), _row_tile(LANES),
         pl.BlockSpec((None, None, CARRY_ROWS, LANES), lambda b, s: (b, s, 0, 0))],
        [jax.ShapeDtypeStruct((bsz, s_len, 2 * d), BF16),
         jax.ShapeDtypeStruct((bsz, n_kv_tiles, d, ATTN_TK), BF16),
         jax.ShapeDtypeStruct((bsz, s_len, LANES), BF16),
         jax.ShapeDtypeStruct((bsz, n_kv_tiles, CARRY_ROWS, LANES), F32)],
        [pltpu.VMEM((CARRY_ROWS, LANES), F32)],
    )(x, row(kv_norm), w_kvf[:, :d].astype(BF16), w_kvf[:, d:2 * d].astype(BF16), w_f3, b_f3,
      row(jnp.tile(k_norm, N_HEADS)), red, bcast, tri, place_k, const_k)
    c_first = c_edges[:, :, 0, :N_HEADS].transpose(0, 2, 1).reshape(-1)
    c_last = c_edges[:, :, 1, :N_HEADS].transpose(0, 2, 1).reshape(-1)

    q_proj = _dense_call(
        _q_kernel, "q_proj", grid,
        [_row_tile(d), _resident((1, d)), _resident((d, d)), _resident((d, d)), _resident((1, d)),
         _row_tile(LANES), _resident((2 * d, LANES)), _resident((2 * LANES, d)),
         _resident((LANES, d)), _resident((1, d))],
        [_row_tile(2 * d), _row_tile(d)],
        [jax.ShapeDtypeStruct((bsz, s_len, 2 * d), BF16), x_shape])

    n_pairs = d // PAIR
    attention = pl.pallas_call(
        _attn_kernel, name="fox_attention",
        grid_spec=pltpu.PrefetchScalarGridSpec(
            num_scalar_prefetch=3, grid=(bsz, n_pairs),
            in_specs=[pl.BlockSpec((None, s_len, 2 * PAIR), lambda b, p, *_: (b, 0, p)),
                      pl.BlockSpec((None, s_len, 2 * PAIR), lambda b, p, *_: (b, 0, p)),
                      pl.BlockSpec((None, n_kv_tiles, PAIR, ATTN_TK), lambda b, p, *_: (b, 0, p, 0)),
                      pl.BlockSpec((None, s_len, PAIR), lambda b, p, *_: (b, 0, p))],
            out_specs=pl.BlockSpec((None, s_len, PAIR), lambda b, p, *_: (b, 0, p)),
            scratch_shapes=[pltpu.VMEM((2, ATTN_TQ, 2 * PAIR), BF16),
                            pltpu.VMEM((SCORE_SLOTS, ATTN_TK, ATTN_TQ), F32),
                            pltpu.VMEM((2, 1, ATTN_TQ), F32), pltpu.VMEM((2, PAIR, ATTN_TQ), F32)]),
        out_shape=jax.ShapeDtypeStruct((bsz, s_len, d), BF16),
        compiler_params=pltpu.CompilerParams(
            dimension_semantics=("arbitrary",) * 2, vmem_limit_bytes=VMEM_LIMIT))
    k_gain = jnp.max(jnp.abs(k_norm))

    def decay_floor(q_gain):
        qk_bound = 1.02 * HEAD_DIM ** 0.5 * jnp.max(jnp.abs(q_gain)) * k_gain
        return (-(2.0 * qk_bound + NEGLIGIBLE_LOG_WEIGHT)).reshape(1).astype(F32)

    layer_b = _dense_call(
        _layer_b_kernel, "layer_b", grid,
        [_row_tile(d), _row_tile(d), _resident((d, d))] + ffn_specs,
        _row_tile(d), x_shape, [ffn_carry])

    scale = HEAD_DIM ** -0.5 * LOG2_E
    for l in range(n_a, depth):
        j = l - n_a
        qcat, gate = q_proj(x, row(attn_norm[l]), b_w_qg[j][:, :d].astype(BF16), b_w_qg[j][:, d:].astype(BF16),
                            row(jnp.tile(q_norm[j], N_HEADS)) * scale, cp, red, bcast, place_q, const_q)
        og = attention(c_first, c_last, decay_floor(q_norm[j]), qcat, kcat, vt, gate)
        x = layer_b(x, og, b_w_out[j].astype(BF16), *ffn_operands(l))
    return x
```

```python
import functools

import jax
import jax.numpy as jnp
from jax import lax
from jax.experimental import pallas as pl
from jax.experimental.pallas import tpu as pltpu

F32 = jnp.float32
BF16 = jnp.bfloat16

EPS = 1e-6
LOG2_E = 1.4426950408889634
N_HEADS = 16
HEAD_DIM = 64
LANES = 128
PAIR = 2 * HEAD_DIM
CARRY_ROWS = 8

ROW_TILE = 512
CHUNK = 256
ATTN_TQ = 512
ATTN_TK = 512
SCORE_SLOTS = 8
DENOM_ROWS = 16
NEGLIGIBLE_LOG_WEIGHT = 88.0
VMEM_LIMIT = 56 * 1024 * 1024


def _dot(a, b):
    return jnp.dot(a, b, preferred_element_type=F32)


def _rms(x, g):
    return x * lax.rsqrt(jnp.mean(x * x, axis=-1, keepdims=True) + EPS) * g


def _split2(x):
    hi = x.astype(BF16)
    lo = (x - hi.astype(F32)).astype(BF16)
    return hi, lo


def _split3(x):
    p1 = x.astype(BF16)
    r1 = x - p1.astype(F32)
    p2 = r1.astype(BF16)
    p3 = (r1 - p2.astype(F32)).astype(BF16)
    return p1, p2, p3


def _head_rms(q, red_ref, bcast_ref):
    sq_hi, sq_lo = _split2(q * q)
    ms = _dot(jnp.concatenate([sq_hi, sq_lo], axis=1), red_ref[...])
    r_hi, r_lo = _split2(lax.rsqrt(ms + EPS))
    return q * _dot(jnp.concatenate([r_hi, r_lo], axis=1), bcast_ref[...])


def _shift_down(a, first_row):
    row = lax.broadcasted_iota(jnp.int32, a.shape, 0)
    return jnp.where(row == 0, first_row, pltpu.roll(a, 1, axis=0))


def _causal_conv3(u, w, carry_ref, lo, hi):
    rows = u.shape[0]
    old = carry_ref[:, lo:hi]
    carry_ref[:, lo:hi] = u[rows - CARRY_ROWS:, :]
    u1 = _shift_down(u, old[CARRY_ROWS - 1:CARRY_ROWS, :])
    u2 = _shift_down(u1, old[CARRY_ROWS - 2:CARRY_ROWS - 1, :])
    return w[2:3, :] * u + w[1:2, :] * u1 + w[0:1, :] * u2


def _mixer(x, gn_ref, w_in_ref, cw_ref, w_out_ref, carry_ref):
    d = x.shape[1]
    xn = _rms(x, gn_ref[...]).astype(BF16)

    def project(j):
        return tuple(_dot(xn, w_in_ref[:, part * d + j * CHUNK:part * d + (j + 1) * CHUNK]) for part in range(3))

    n_chunks = d // CHUNK
    acc = jnp.zeros(x.shape, F32)
    ahead = project(0)
    for j in range(n_chunks):
        lo, hi = j * CHUNK, (j + 1) * CHUNK
        b, c, h = ahead
        if j + 1 < n_chunks:
            ahead = project(j + 1)
        u = _causal_conv3(c * h, cw_ref[:, lo:hi], carry_ref, lo, hi)
        acc = acc + _dot((b * u).astype(BF16), w_out_ref[lo:hi, :])
    return x + acc


def _conv_ffn(x, gn_ref, w_up_ref, cw_ref, w_down_ref, carry_ref):
    f = w_down_ref.shape[0]
    xn = _rms(x, gn_ref[...]).astype(BF16)

    def project(j):
        return tuple(_dot(xn, w_up_ref[:, part * f + j * CHUNK:part * f + (j + 1) * CHUNK]) for part in range(2))

    n_chunks = f // CHUNK
    acc = jnp.zeros(x.shape, F32)
    ahead = project(0)
    for j in range(n_chunks):
        lo, hi = j * CHUNK, (j + 1) * CHUNK
        a, g = ahead
        if j + 1 < n_chunks:
            ahead = project(j + 1)
        a = _causal_conv3(a, cw_ref[:, lo:hi], carry_ref, lo, hi)
        hid = a * jax.nn.sigmoid(a) * g
        acc = acc + _dot(hid.astype(BF16), w_down_ref[lo:hi, :])
    return x + acc


def _row_halves(rows):
    return [pl.ds(0, rows // 2), pl.ds(rows // 2, rows // 2)]


def _reset_at_sequence_start(*carry_refs):
    @pl.when(pl.program_id(1) == 0)
    def _():
        for ref in carry_refs:
            ref[...] = jnp.zeros(ref.shape, ref.dtype)


def _layer_a_kernel(x_ref, an_ref, w_in_ref, acw_ref, w_out_ref,
                    fn_ref, w_up_ref, fcw_ref, w_down_ref, o_ref,
                    mix_carry, ffn_carry):
    _reset_at_sequence_start(mix_carry, ffn_carry)
    halves = _row_halves(x_ref.shape[0])
    mixed = [_mixer(x_ref[rows, :], an_ref, w_in_ref, acw_ref, w_out_ref, mix_carry) for rows in halves]
    for rows, x in zip(halves, mixed):
        o_ref[rows, :] = _conv_ffn(x, fn_ref, w_up_ref, fcw_ref, w_down_ref, ffn_carry)


def _layer_b_kernel(x_ref, og_ref, w_out_ref,
                    fn_ref, w_up_ref, fcw_ref, w_down_ref, o_ref, ffn_carry):
    _reset_at_sequence_start(ffn_carry)
    halves = _row_halves(x_ref.shape[0])
    attended = [x_ref[rows, :] + _dot(og_ref[rows, :], w_out_ref[...]) for rows in halves]
    for rows, x in zip(halves, attended):
        o_ref[rows, :] = _conv_ffn(x, fn_ref, w_up_ref, fcw_ref, w_down_ref, ffn_carry)


def _interleave_pairs(main, extra, out_ref, rows):
    for p in range(main.shape[1] // PAIR):
        out_ref[rows, 2 * p * PAIR:(2 * p + 1) * PAIR] = main[:, p * PAIR:(p + 1) * PAIR].astype(out_ref.dtype)
        out_ref[rows, (2 * p + 1) * PAIR:(2 * p + 2) * PAIR] = extra[:, p * PAIR:(p + 1) * PAIR].astype(out_ref.dtype)


def _kv_kernel(x_ref, gn_ref, wk_ref, wv_ref, wf_ref, bf_ref, kg_ref,
               red_ref, bcast_ref, tri_ref, place_ref, const_ref,
               kcat_ref, vt_ref, cp_ref, edge_ref, c_carry):
    _reset_at_sequence_start(c_carry)
    halves = _row_halves(x_ref.shape[0])
    half = x_ref.shape[0] // 2
    tri = tri_ref[:half, :half]
    edges = []
    for rows in halves:
        h = _rms(x_ref[rows, :], gn_ref[...]).astype(BF16)
        vt_ref[:, rows] = _dot(h, wv_ref[...]).T.astype(vt_ref.dtype)

        f_logit = _dot(h, wf_ref[...]) + bf_ref[...]
        log_f = jnp.minimum(f_logit, 0.0) - jnp.log1p(jnp.exp(-jnp.abs(f_logit)))
        c = c_carry[0:1, :] + sum(_dot(tri, piece) for piece in _split3(log_f))
        last = c[half - 1:, :]
        c_carry[0:1, :] = last
        edges += [c[0:1, :], last]
        c1, c2, c3 = _split3(c * LOG2_E)
        lane = lax.broadcasted_iota(jnp.int32, c.shape, 1)
        cp = jnp.where(lane < N_HEADS, c1, jnp.where(lane < 2 * N_HEADS, c2, c3))
        cp_ref[rows, :] = cp

        k = _head_rms(_dot(h, wk_ref[...]), red_ref, bcast_ref) * kg_ref[...]
        extra = _dot(cp, place_ref[...]) + const_ref[...]
        _interleave_pairs(k, extra, kcat_ref, rows)
    edge_row = lax.broadcasted_iota(jnp.int32, edge_ref.shape, 0)
    edge_ref[...] = jnp.where(edge_row == 0, edges[0], jnp.where(edge_row == 1, edges[-1], 0.0))


def _q_kernel(x_ref, gn_ref, wq_ref, wg_ref, qg_ref, cp_ref,
              red_ref, bcast_ref, place_ref, const_ref,
              qcat_ref, gate_ref):
    for rows in _row_halves(x_ref.shape[0]):
        xn = _rms(x_ref[rows, :], gn_ref[...]).astype(BF16)
        gate_ref[rows, :] = jax.nn.sigmoid(_dot(xn, wg_ref[...]))
        q = _head_rms(_dot(xn, wq_ref[...]), red_ref, bcast_ref) * qg_ref[...]
        extra = _dot(cp_ref[rows, :], place_ref[...]) + const_ref[...]
        _interleave_pairs(q, extra, qcat_ref, rows)


def _attn_kernel(c_first_ref, c_last_ref, floor_ref, q_ref, k_ref, vt_ref, gate_ref, o_ref,
                 qh_scr, s_scr, m_scr, acc_scr):
    n_tiles = q_ref.shape[0] // ATTN_TQ
    pl.loop(0, n_tiles)(functools.partial(
        _attend_query_tile, c_first_ref, c_last_ref, floor_ref, q_ref, k_ref, vt_ref, gate_ref, o_ref,
        qh_scr, s_scr, m_scr, acc_scr, n_tiles))


def _attend_query_tile(c_first_ref, c_last_ref, floor_ref, q_ref, k_ref, vt_ref, gate_ref, o_ref,
                       qh_scr, s_scr, m_scr, acc_scr, n_tiles, qi):
    b, pair = pl.program_id(0), pl.program_id(1)
    decay_floor = floor_ref[0]
    q_rows = pl.ds(pl.multiple_of(qi * ATTN_TQ, ATTN_TQ), ATTN_TQ)

    def first_needed_tile(head):
        base = (b * N_HEADS + 2 * pair + head) * n_tiles
        c_query = c_first_ref[base + qi]
        return lax.fori_loop(
            0, qi, lambda j, n: n + jnp.where(c_query - c_last_ref[base + j] < decay_floor, 1, 0), 0)

    first0, first1 = first_needed_tile(0), first_needed_tile(1)
    odd = (first0 + first1) % 2
    first0, first1 = (first0 - jnp.where((odd == 1) & (first0 > 0), 1, 0),
                      first1 - jnp.where((odd == 1) & (first0 == 0), 1, 0))
    n0, n1 = qi - first0, qi - first1
    n_full = n0 + n1

    def item(u):
        in0, in1 = u < n0, u < n_full
        return jnp.where(in0, 0, jnp.where(in1, 1, 0)), jnp.where(in0, first0 + u, jnp.where(in1, first1 + u - n0, qi))

    q = q_ref[q_rows, :]
    lane_q = lax.broadcasted_iota(jnp.int32, (1, q.shape[1]), 1)
    for head in range(2):
        qh_scr[head] = jnp.where((lane_q % PAIR) // HEAD_DIM == head, q, jnp.zeros_like(q))
    ones = jnp.ones((DENOM_ROWS, ATTN_TK), BF16)

    def produce(head, j, slot):
        start = pl.multiple_of(j * ATTN_TK, ATTN_TK)
        s_scr[slot] = lax.dot_general(k_ref[pl.ds(start, ATTN_TK), :], qh_scr[head], (((1,), (1,)), ((), ())),
                                      preferred_element_type=F32)

    def consume(head, j, slot, on_diagonal):
        s = s_scr[slot]
        if on_diagonal:
            key_pos = lax.broadcasted_iota(jnp.int32, s.shape, 0)
            query_pos = lax.broadcasted_iota(jnp.int32, s.shape, 1)
            s = jnp.where(key_pos <= query_pos, s, -jnp.inf)
        m = m_scr[head]
        m_new = jnp.maximum(m, jnp.max(s, axis=0, keepdims=True))
        p = jnp.exp2(s - m_new).astype(BF16)
        v_rows = pl.ds(pl.multiple_of(head * HEAD_DIM, HEAD_DIM), HEAD_DIM)
        v_aug = jnp.concatenate([vt_ref[j, v_rows, :], ones], axis=0)
        acc_scr[head] = jnp.exp2(m - m_new) * acc_scr[head] + _dot(v_aug, p)
        m_scr[head] = m_new

    def pipelined(u, n_items):
        for k in range(n_items):
            produce(*item(u + k + 1), (k + 1) % n_items)
            consume(*item(u + k), k, False)

    m_scr[...] = jnp.full(m_scr.shape, -jnp.inf, F32)
    acc_scr[...] = jnp.zeros(acc_scr.shape, F32)
    produce(*item(0), 0)

    @pl.loop(0, n_full // SCORE_SLOTS)
    def _(i):
        pipelined(SCORE_SLOTS * i, SCORE_SLOTS)

    done = n_full - n_full % SCORE_SLOTS
    run = SCORE_SLOTS // 2
    while run >= 2:
        @pl.when(n_full & run != 0)
        def _(done=done, run=run):
            pipelined(done, run)
        done = done + (n_full & run)
        run //= 2

    produce(1, qi, 1)
    consume(0, qi, 0, True)
    consume(1, qi, 1, True)

    o_t = jnp.concatenate(
        [acc_scr[head, :HEAD_DIM, :] / acc_scr[head, HEAD_DIM:HEAD_DIM + 1, :] for head in range(2)], axis=0)
    o_ref[q_rows, :] = (o_t.T * gate_ref[q_rows, :]).astype(o_ref.dtype)


def _resident(shape):
    return pl.BlockSpec(shape, lambda *_: (0,) * len(shape), pipeline_mode=pl.Buffered(1))


def _row_tile(cols):
    return pl.BlockSpec((None, ROW_TILE, cols), lambda b, s: (b, s, 0))


def _dense_call(kernel, name, grid, in_specs, out_specs, out_shape, scratch_shapes=()):
    return pl.pallas_call(
        kernel, name=name, grid=grid, in_specs=in_specs, out_specs=out_specs, out_shape=out_shape,
        scratch_shapes=list(scratch_shapes),
        compiler_params=pltpu.CompilerParams(
            dimension_semantics=("arbitrary", "arbitrary"), vmem_limit_bytes=VMEM_LIMIT))


def _placement_constants():
    d = N_HEADS * HEAD_DIM
    head_of_col = jnp.arange(d) // HEAD_DIM
    pos_in_head = jnp.arange(d) % HEAD_DIM
    lane = jnp.arange(LANES)
    red = (head_of_col[:, None] == lane[None, :]).astype(F32) / HEAD_DIM
    red = jnp.concatenate([red, red], axis=0).astype(BF16)
    bcast = (lane[:, None] == head_of_col[None, :]).astype(BF16)
    bcast = jnp.concatenate([bcast, bcast], axis=0)
    piece, head = lane // N_HEADS, lane % N_HEADS
    valid = (piece < 3)[:, None] & (head[:, None] == head_of_col[None, :])
    place_q = (valid & (pos_in_head[None, :] == piece[:, None])).astype(BF16)
    place_k = -(valid & (pos_in_head[None, :] == 3 + piece[:, None])).astype(BF16)
    const_q = ((pos_in_head >= 3) & (pos_in_head < 6)).astype(F32)[None, :]
    const_k = (pos_in_head < 3).astype(F32)[None, :]
    return red, bcast, place_q, const_q, place_k, const_k


def kernel(x, attn_norm, ffn_norm, a_w_in, a_conv, a_w_out, kv_norm, w_kvf, b_f, k_norm,
           b_w_qg, q_norm, b_w_out, ffn_w_up, ffn_conv, ffn_w_down):
    bsz, s_len, d = x.shape
    depth = ffn_w_up.shape[0]
    n_a = a_w_in.shape[0]
    f = ffn_w_down.shape[1]
    assert d == N_HEADS * HEAD_DIM and s_len % ROW_TILE == 0 and s_len % ATTN_TQ == 0
    assert ATTN_TQ == ATTN_TK == ROW_TILE and d % CHUNK == 0 and f % CHUNK == 0

    grid = (bsz, s_len // ROW_TILE)
    x_shape = jax.ShapeDtypeStruct((bsz, s_len, d), F32)
    red, bcast, place_q, const_q, place_k, const_k = _placement_constants()
    row = lambda v: v.reshape(1, -1)

    def ffn_operands(l):
        return (row(ffn_norm[l]), ffn_w_up[l].astype(BF16), ffn_conv[l], ffn_w_down[l].astype(BF16))

    ffn_specs = [_resident((1, d)), _resident((d, 2 * f)), _resident((3, f)), _resident((f, d))]
    ffn_carry = pltpu.VMEM((CARRY_ROWS, f), F32)

    layer_a = _dense_call(
        _layer_a_kernel, "layer_a", grid,
        [_row_tile(d), _resident((1, d)), _resident((d, 3 * d)), _resident((3, d)), _resident((d, d))] + ffn_specs,
        _row_tile(d), x_shape, [pltpu.VMEM((CARRY_ROWS, d), F32), ffn_carry])
    for l in range(n_a):
        x = layer_a(x, row(attn_norm[l]), a_w_in[l].astype(BF16), a_conv[l], a_w_out[l].astype(BF16),
                    *ffn_operands(l))

    w_f = w_kvf[:, 2 * d:]
    pad = LANES - 3 * N_HEADS
    w_f3 = jnp.pad(jnp.concatenate([w_f, w_f, w_f], axis=1), ((0, 0), (0, pad))).astype(BF16)
    b_f3 = jnp.pad(jnp.concatenate([b_f, b_f, b_f]), (0, pad)).reshape(1, LANES)
    tri = (jnp.arange(ROW_TILE)[:, None] >= jnp.arange(ROW_TILE)[None, :]).astype(BF16)
    n_kv_tiles = s_len // ATTN_TK
    kcat, vt, cp, c_edges = _dense_call(
        _kv_kernel, "kv_proj", grid,
        [_row_tile(d), _resident((1, d)), _resident((d, d)), _resident((d, d)), _resident((d, LANES)),
         _resident((1, LANES)), _resident((1, d)), _resident((2 * d, LANES)), _resident((2 * LANES, d)),
         _resident((ROW_TILE, ROW_TILE)), _resident((LANES, d)), _resident((1, d))],
        [_row_tile(2 * d), pl.BlockSpec((None, None, d, ATTN_TK), lambda b, s: (b, s, 0, 0)), _row_tile(LANES),
         pl.BlockSpec((None, None, CARRY_ROWS, LANES), lambda b, s: (b, s, 0, 0))],
        [jax.ShapeDtypeStruct((bsz, s_len, 2 * d), BF16),
         jax.ShapeDtypeStruct((bsz, n_kv_tiles, d, ATTN_TK), BF16),
         jax.ShapeDtypeStruct((bsz, s_len, LANES), BF16),
         jax.ShapeDtypeStruct((bsz, n_kv_tiles, CARRY_ROWS, LANES), F32)],
        [pltpu.VMEM((CARRY_ROWS, LANES), F32)],
    )(x, row(kv_norm), w_kvf[:, :d].astype(BF16), w_kvf[:, d:2 * d].astype(BF16), w_f3, b_f3,
      row(jnp.tile(k_norm, N_HEADS)), red, bcast, tri, place_k, const_k)
    c_first = c_edges[:, :, 0, :N_HEADS].transpose(0, 2, 1).reshape(-1)
    c_last = c_edges[:, :, 1, :N_HEADS].transpose(0, 2, 1).reshape(-1)

    q_proj = _dense_call(
        _q_kernel, "q_proj", grid,
        [_row_tile(d), _resident((1, d)), _resident((d, d)), _resident((d, d)), _resident((1, d)),
         _row_tile(LANES), _resident((2 * d, LANES)), _resident((2 * LANES, d)),
         _resident((LANES, d)), _resident((1, d))],
        [_row_tile(2 * d), _row_tile(d)],
        [jax.ShapeDtypeStruct((bsz, s_len, 2 * d), BF16), x_shape])

    n_pairs = d // PAIR
    attention = pl.pallas_call(
        _attn_kernel, name="fox_attention",
        grid_spec=pltpu.PrefetchScalarGridSpec(
            num_scalar_prefetch=3, grid=(bsz, n_pairs),
            in_specs=[pl.BlockSpec((None, s_len, 2 * PAIR), lambda b, p, *_: (b, 0, p)),
                      pl.BlockSpec((None, s_len, 2 * PAIR), lambda b, p, *_: (b, 0, p)),
                      pl.BlockSpec((None, n_kv_tiles, PAIR, ATTN_TK), lambda b, p, *_: (b, 0, p, 0)),
                      pl.BlockSpec((None, s_len, PAIR), lambda b, p, *_: (b, 0, p))],
            out_specs=pl.BlockSpec((None, s_len, PAIR), lambda b, p, *_: (b, 0, p)),
            scratch_shapes=[pltpu.VMEM((2, ATTN_TQ, 2 * PAIR), BF16),
                            pltpu.VMEM((SCORE_SLOTS, ATTN_TK, ATTN_TQ), F32),
                            pltpu.VMEM((2, 1, ATTN_TQ), F32),
                            pltpu.VMEM((2, HEAD_DIM + DENOM_ROWS, ATTN_TQ), F32)]),
        out_shape=jax.ShapeDtypeStruct((bsz, s_len, d), BF16),
        compiler_params=pltpu.CompilerParams(
            dimension_semantics=("arbitrary",) * 2, vmem_limit_bytes=VMEM_LIMIT))
    k_gain = jnp.max(jnp.abs(k_norm))

    def decay_floor(q_gain):
        qk_bound = 1.02 * HEAD_DIM ** 0.5 * jnp.max(jnp.abs(q_gain)) * k_gain
        return (-(2.0 * qk_bound + NEGLIGIBLE_LOG_WEIGHT)).reshape(1).astype(F32)

    layer_b = _dense_call(
        _layer_b_kernel, "layer_b", grid,
        [_row_tile(d), _row_tile(d), _resident((d, d))] + ffn_specs,
        _row_tile(d), x_shape, [ffn_carry])

    scale = HEAD_DIM ** -0.5 * LOG2_E
    for l in range(n_a, depth):
        j = l - n_a
        qcat, gate = q_proj(x, row(attn_norm[l]), b_w_qg[j][:, :d].astype(BF16), b_w_qg[j][:, d:].astype(BF16),
                            row(jnp.tile(q_norm[j], N_HEADS)) * scale, cp, red, bcast, place_q, const_q)
        og = attention(c_first, c_last, decay_floor(q_norm[j]), qcat, kcat, vt, gate)
        x = layer_b(x, og, b_w_out[j].astype(BF16), *ffn_operands(l))
    return x
```

```python
import functools

import jax
import jax.numpy as jnp
from jax import lax
from jax.experimental import pallas as pl
from jax.experimental.pallas import tpu as pltpu

F32 = jnp.float32
BF16 = jnp.bfloat16

EPS = 1e-6
LOG2_E = 1.4426950408889634
N_HEADS = 16
HEAD_DIM = 64
LANES = 128
PAIR = 2 * HEAD_DIM
CARRY_ROWS = 8

ROW_TILE = 512
CHUNK = 256
ATTN_TQ = 512
ATTN_TK = 512
SCORE_SLOTS = 8
SCORES_AHEAD = 2
NEGLIGIBLE_LOG_WEIGHT = 88.0
VMEM_LIMIT = 56 * 1024 * 1024


def _dot(a, b):
    return jnp.dot(a, b, preferred_element_type=F32)


def _rms(x, g):
    return x * lax.rsqrt(jnp.mean(x * x, axis=-1, keepdims=True) + EPS) * g


def _split2(x):
    hi = x.astype(BF16)
    lo = (x - hi.astype(F32)).astype(BF16)
    return hi, lo


def _split3(x):
    p1 = x.astype(BF16)
    r1 = x - p1.astype(F32)
    p2 = r1.astype(BF16)
    p3 = (r1 - p2.astype(F32)).astype(BF16)
    return p1, p2, p3


def _head_rms(q, red_ref, bcast_ref):
    sq_hi, sq_lo = _split2(q * q)
    ms = _dot(jnp.concatenate([sq_hi, sq_lo], axis=1), red_ref[...])
    r_hi, r_lo = _split2(lax.rsqrt(ms + EPS))
    return q * _dot(jnp.concatenate([r_hi, r_lo], axis=1), bcast_ref[...])


def _shift_down(a, first_row):
    row = lax.broadcasted_iota(jnp.int32, a.shape, 0)
    return jnp.where(row == 0, first_row, pltpu.roll(a, 1, axis=0))


def _causal_conv3(u, w, carry_ref, lo, hi):
    rows = u.shape[0]
    old = carry_ref[:, lo:hi]
    carry_ref[:, lo:hi] = u[rows - CARRY_ROWS:, :]
    u1 = _shift_down(u, old[CARRY_ROWS - 1:CARRY_ROWS, :])
    u2 = _shift_down(u1, old[CARRY_ROWS - 2:CARRY_ROWS - 1, :])
    return w[2:3, :] * u + w[1:2, :] * u1 + w[0:1, :] * u2


def _mixer(x, gn_ref, w_in_ref, cw_ref, w_out_ref, carry_ref):
    d = x.shape[1]
    xn = _rms(x, gn_ref[...]).astype(BF16)

    def project(j):
        return tuple(_dot(xn, w_in_ref[:, part * d + j * CHUNK:part * d + (j + 1) * CHUNK]) for part in range(3))

    n_chunks = d // CHUNK
    acc = jnp.zeros(x.shape, F32)
    ahead = project(0)
    for j in range(n_chunks):
        lo, hi = j * CHUNK, (j + 1) * CHUNK
        b, c, h = ahead
        if j + 1 < n_chunks:
            ahead = project(j + 1)
        u = _causal_conv3(c * h, cw_ref[:, lo:hi], carry_ref, lo, hi)
        acc = acc + _dot((b * u).astype(BF16), w_out_ref[lo:hi, :])
    return x + acc


def _conv_ffn(x, gn_ref, w_up_ref, cw_ref, w_down_ref, carry_ref):
    f = w_down_ref.shape[0]
    xn = _rms(x, gn_ref[...]).astype(BF16)

    def project(j):
        return tuple(_dot(xn, w_up_ref[:, part * f + j * CHUNK:part * f + (j + 1) * CHUNK]) for part in range(2))

    n_chunks = f // CHUNK
    acc = jnp.zeros(x.shape, F32)
    ahead = project(0)
    for j in range(n_chunks):
        lo, hi = j * CHUNK, (j + 1) * CHUNK
        a, g = ahead
        if j + 1 < n_chunks:
            ahead = project(j + 1)
        a = _causal_conv3(a, cw_ref[:, lo:hi], carry_ref, lo, hi)
        hid = a * jax.nn.sigmoid(a) * g
        acc = acc + _dot(hid.astype(BF16), w_down_ref[lo:hi, :])
    return x + acc


def _row_halves(rows):
    return [pl.ds(0, rows // 2), pl.ds(rows // 2, rows // 2)]


def _reset_at_sequence_start(*carry_refs):
    @pl.when(pl.program_id(1) == 0)
    def _():
        for ref in carry_refs:
            ref[...] = jnp.zeros(ref.shape, ref.dtype)


def _layer_a_kernel(x_ref, an_ref, w_in_ref, acw_ref, w_out_ref,
                    fn_ref, w_up_ref, fcw_ref, w_down_ref, o_ref,
                    mix_carry, ffn_carry):
    _reset_at_sequence_start(mix_carry, ffn_carry)
    halves = _row_halves(x_ref.shape[0])
    mixed = [_mixer(x_ref[rows, :], an_ref, w_in_ref, acw_ref, w_out_ref, mix_carry) for rows in halves]
    for rows, x in zip(halves, mixed):
        o_ref[rows, :] = _conv_ffn(x, fn_ref, w_up_ref, fcw_ref, w_down_ref, ffn_carry)


def _layer_b_kernel(x_ref, og_ref, w_out_ref,
                    fn_ref, w_up_ref, fcw_ref, w_down_ref, o_ref, ffn_carry):
    _reset_at_sequence_start(ffn_carry)
    halves = _row_halves(x_ref.shape[0])
    attended = [x_ref[rows, :] + _dot(og_ref[rows, :], w_out_ref[...]) for rows in halves]
    for rows, x in zip(halves, attended):
        o_ref[rows, :] = _conv_ffn(x, fn_ref, w_up_ref, fcw_ref, w_down_ref, ffn_carry)


def _interleave_pairs(main, extra, out_ref, rows):
    for p in range(main.shape[1] // PAIR):
        out_ref[rows, 2 * p * PAIR:(2 * p + 1) * PAIR] = main[:, p * PAIR:(p + 1) * PAIR].astype(out_ref.dtype)
        out_ref[rows, (2 * p + 1) * PAIR:(2 * p + 2) * PAIR] = extra[:, p * PAIR:(p + 1) * PAIR].astype(out_ref.dtype)


def _kv_kernel(x_ref, gn_ref, wk_ref, wv_ref, wf_ref, bf_ref, kg_ref,
               red_ref, bcast_ref, tri_ref, place_ref, const_ref,
               kcat_ref, vt_ref, cp_ref, edge_ref, c_carry):
    _reset_at_sequence_start(c_carry)
    halves = _row_halves(x_ref.shape[0])
    half = x_ref.shape[0] // 2
    tri = tri_ref[:half, :half]
    edges = []
    for rows in halves:
        h = _rms(x_ref[rows, :], gn_ref[...]).astype(BF16)
        vt_ref[:, rows] = _dot(h, wv_ref[...]).T.astype(vt_ref.dtype)

        f_logit = _dot(h, wf_ref[...]) + bf_ref[...]
        log_f = jnp.minimum(f_logit, 0.0) - jnp.log1p(jnp.exp(-jnp.abs(f_logit)))
        c = c_carry[0:1, :] + sum(_dot(tri, piece) for piece in _split3(log_f))
        last = c[half - 1:, :]
        c_carry[0:1, :] = last
        edges += [c[0:1, :], last]
        c1, c2, c3 = _split3(c * LOG2_E)
        lane = lax.broadcasted_iota(jnp.int32, c.shape, 1)
        cp = jnp.where(lane < N_HEADS, c1, jnp.where(lane < 2 * N_HEADS, c2, c3))
        cp_ref[rows, :] = cp

        k = _head_rms(_dot(h, wk_ref[...]), red_ref, bcast_ref) * kg_ref[...]
        extra = _dot(cp, place_ref[...]) + const_ref[...]
        _interleave_pairs(k, extra, kcat_ref, rows)
    edge_row = lax.broadcasted_iota(jnp.int32, edge_ref.shape, 0)
    edge_ref[...] = jnp.where(edge_row == 0, edges[0], jnp.where(edge_row == 1, edges[-1], 0.0))


def _q_kernel(x_ref, gn_ref, wq_ref, wg_ref, qg_ref, cp_ref,
              red_ref, bcast_ref, place_ref, const_ref,
              qcat_ref, gate_ref):
    for rows in _row_halves(x_ref.shape[0]):
        xn = _rms(x_ref[rows, :], gn_ref[...]).astype(BF16)
        gate_ref[rows, :] = jax.nn.sigmoid(_dot(xn, wg_ref[...]))
        q = _head_rms(_dot(xn, wq_ref[...]), red_ref, bcast_ref) * qg_ref[...]
        extra = _dot(cp_ref[rows, :], place_ref[...]) + const_ref[...]
        _interleave_pairs(q, extra, qcat_ref, rows)


def _attn_kernel(c_first_ref, c_last_ref, floor_ref, q_ref, k_ref, vt_ref, gate_ref, o_ref,
                 qh_scr, s_scr, m_scr, acc_scr):
    n_tiles = q_ref.shape[0] // ATTN_TQ
    pl.loop(0, n_tiles)(functools.partial(
        _attend_query_tile, c_first_ref, c_last_ref, floor_ref, q_ref, k_ref, vt_ref, gate_ref, o_ref,
        qh_scr, s_scr, m_scr, acc_scr, n_tiles))


def _attend_query_tile(c_first_ref, c_last_ref, floor_ref, q_ref, k_ref, vt_ref, gate_ref, o_ref,
                       qh_scr, s_scr, m_scr, acc_scr, n_tiles, qi):
    b, pair = pl.program_id(0), pl.program_id(1)
    decay_floor = floor_ref[0]
    q_rows = pl.ds(pl.multiple_of(qi * ATTN_TQ, ATTN_TQ), ATTN_TQ)

    def first_needed_tile(head):
        base = (b * N_HEADS + 2 * pair + head) * n_tiles
        c_query = c_first_ref[base + qi]
        return lax.fori_loop(
            0, qi, lambda j, n: n + jnp.where(c_query - c_last_ref[base + j] < decay_floor, 1, 0), 0)

    first0, first1 = first_needed_tile(0), first_needed_tile(1)
    odd = (first0 + first1) % 2
    first0, first1 = (first0 - jnp.where((odd == 1) & (first0 > 0), 1, 0),
                      first1 - jnp.where((odd == 1) & (first0 == 0), 1, 0))
    n0, n1 = qi - first0, qi - first1
    n_full = n0 + n1

    def item(u):
        in0, in1 = u < n0, u < n_full
        return jnp.where(in0, 0, jnp.where(in1, 1, 0)), jnp.where(in0, first0 + u, jnp.where(in1, first1 + u - n0, qi))

    q = q_ref[q_rows, :]
    lane_q = lax.broadcasted_iota(jnp.int32, (1, q.shape[1]), 1)
    for head in range(2):
        qh_scr[head] = jnp.where((lane_q % PAIR) // HEAD_DIM == head, q, jnp.zeros_like(q))
    ones = jnp.ones((HEAD_DIM, ATTN_TK), BF16)

    def produce(head, j, slot):
        start = pl.multiple_of(j * ATTN_TK, ATTN_TK)
        s_scr[slot] = lax.dot_general(k_ref[pl.ds(start, ATTN_TK), :], qh_scr[head], (((1,), (1,)), ((), ())),
                                      preferred_element_type=F32)

    def consume(head, j, slot, on_diagonal):
        s = s_scr[slot]
        if on_diagonal:
            key_pos = lax.broadcasted_iota(jnp.int32, s.shape, 0)
            query_pos = lax.broadcasted_iota(jnp.int32, s.shape, 1)
            s = jnp.where(key_pos <= query_pos, s, -jnp.inf)
        m = m_scr[head]
        m_new = jnp.maximum(m, jnp.max(s, axis=0, keepdims=True))
        p = jnp.exp2(s - m_new).astype(BF16)
        v_rows = pl.ds(pl.multiple_of(head * HEAD_DIM, HEAD_DIM), HEAD_DIM)
        v_aug = jnp.concatenate([vt_ref[j, v_rows, :], ones], axis=0)
        acc_scr[head] = jnp.exp2(m - m_new) * acc_scr[head] + _dot(v_aug, p)
        m_scr[head] = m_new

    def pipelined(u, n_items):
        produced = 0
        for k in range(n_items):
            target = min(k + SCORES_AHEAD, n_items if k >= 1 else n_items - 1)
            while produced < target:
                produced += 1
                produce(*item(u + produced), produced % n_items)
            consume(*item(u + k), k, False)

    m_scr[...] = jnp.full(m_scr.shape, -jnp.inf, F32)
    acc_scr[...] = jnp.zeros(acc_scr.shape, F32)
    produce(*item(0), 0)

    @pl.loop(0, n_full // SCORE_SLOTS)
    def _(i):
        pipelined(SCORE_SLOTS * i, SCORE_SLOTS)

    done = n_full - n_full % SCORE_SLOTS
    run = SCORE_SLOTS // 2
    while run >= 2:
        @pl.when(n_full & run != 0)
        def _(done=done, run=run):
            pipelined(done, run)
        done = done + (n_full & run)
        run //= 2

    produce(1, qi, 1)
    consume(0, qi, 0, True)
    consume(1, qi, 1, True)

    o_t = jnp.concatenate([acc_scr[head, :HEAD_DIM, :] / acc_scr[head, HEAD_DIM:, :] for head in range(2)], axis=0)
    o_ref[q_rows, :] = (o_t.T * gate_ref[q_rows, :]).astype(o_ref.dtype)


def _resident(shape):
    return pl.BlockSpec(shape, lambda *_: (0,) * len(shape), pipeline_mode=pl.Buffered(1))


def _row_tile(cols):
    return pl.BlockSpec((None, ROW_TILE, cols), lambda b, s: (b, s, 0))


def _dense_call(kernel, name, grid, in_specs, out_specs, out_shape, scratch_shapes=()):
    return pl.pallas_call(
        kernel, name=name, grid=grid, in_specs=in_specs, out_specs=out_specs, out_shape=out_shape,
        scratch_shapes=list(scratch_shapes),
        compiler_params=pltpu.CompilerParams(
            dimension_semantics=("arbitrary", "arbitrary"), vmem_limit_bytes=VMEM_LIMIT))


def _placement_constants():
    d = N_HEADS * HEAD_DIM
    head_of_col = jnp.arange(d) // HEAD_DIM
    pos_in_head = jnp.arange(d) % HEAD_DIM
    lane = jnp.arange(LANES)
    red = (head_of_col[:, None] == lane[None, :]).astype(F32) / HEAD_DIM
    red = jnp.concatenate([red, red], axis=0).astype(BF16)
    bcast = (lane[:, None] == head_of_col[None, :]).astype(BF16)
    bcast = jnp.concatenate([bcast, bcast], axis=0)
    piece, head = lane // N_HEADS, lane % N_HEADS
    valid = (piece < 3)[:, None] & (head[:, None] == head_of_col[None, :])
    place_q = (valid & (pos_in_head[None, :] == piece[:, None])).astype(BF16)
    place_k = -(valid & (pos_in_head[None, :] == 3 + piece[:, None])).astype(BF16)
    const_q = ((pos_in_head >= 3) & (pos_in_head < 6)).astype(F32)[None, :]
    const_k = (pos_in_head < 3).astype(F32)[None, :]
    return red, bcast, place_q, const_q, place_k, const_k


def kernel(x, attn_norm, ffn_norm, a_w_in, a_conv, a_w_out, kv_norm, w_kvf, b_f, k_norm,
           b_w_qg, q_norm, b_w_out, ffn_w_up, ffn_conv, ffn_w_down):
    bsz, s_len, d = x.shape
    depth = ffn_w_up.shape[0]
    n_a = a_w_in.shape[0]
    f = ffn_w_down.shape[1]
    assert d == N_HEADS * HEAD_DIM and s_len % ROW_TILE == 0 and s_len % ATTN_TQ == 0
    assert ATTN_TQ == ATTN_TK == ROW_TILE and d % CHUNK == 0 and f % CHUNK == 0

    grid = (bsz, s_len // ROW_TILE)
    x_shape = jax.ShapeDtypeStruct((bsz, s_len, d), F32)
    red, bcast, place_q, const_q, place_k, const_k = _placement_constants()
    row = lambda v: v.reshape(1, -1)

    def ffn_operands(l):
        return (row(ffn_norm[l]), ffn_w_up[l].astype(BF16), ffn_conv[l], ffn_w_down[l].astype(BF16))

    ffn_specs = [_resident((1, d)), _resident((d, 2 * f)), _resident((3, f)), _resident((f, d))]
    ffn_carry = pltpu.VMEM((CARRY_ROWS, f), F32)

    layer_a = _dense_call(
        _layer_a_kernel, "layer_a", grid,
        [_row_tile(d), _resident((1, d)), _resident((d, 3 * d)), _resident((3, d)), _resident((d, d))] + ffn_specs,
        _row_tile(d), x_shape, [pltpu.VMEM((CARRY_ROWS, d), F32), ffn_carry])
    for l in range(n_a):
        x = layer_a(x, row(attn_norm[l]), a_w_in[l].astype(BF16), a_conv[l], a_w_out[l].astype(BF16),
                    *ffn_operands(l))

    w_f = w_kvf[:, 2 * d:]
    pad = LANES - 3 * N_HEADS
    w_f3 = jnp.pad(jnp.concatenate([w_f, w_f, w_f], axis=1), ((0, 0), (0, pad))).astype(BF16)
    b_f3 = jnp.pad(jnp.concatenate([b_f, b_f, b_f]), (0, pad)).reshape(1, LANES)
    tri = (jnp.arange(ROW_TILE)[:, None] >= jnp.arange(ROW_TILE)[None, :]).astype(BF16)
    n_kv_tiles = s_len // ATTN_TK
    kcat, vt, cp, c_edges = _dense_call(
        _kv_kernel, "kv_proj", grid,
        [_row_tile(d), _resident((1, d)), _resident((d, d)), _resident((d, d)), _resident((d, LANES)),
         _resident((1, LANES)), _resident((1, d)), _resident((2 * d, LANES)), _resident((2 * LANES, d)),
         _resident((ROW_TILE, ROW_TILE)), _resident((LANES, d)), _resident((1, d))],
        [_row_tile(2 * d), pl.BlockSpec((None, None, d, ATTN_TK), lambda b, s: (b, s, 0, 0)), _row_tile(LANES),
         pl.BlockSpec((None, None, CARRY_ROWS, LANES), lambda b, s: (b, s, 0, 0))],
        [jax.ShapeDtypeStruct((bsz, s_len, 2 * d), BF16),
         jax.ShapeDtypeStruct((bsz, n_kv_tiles, d, ATTN_TK), BF16),
         jax.ShapeDtypeStruct((bsz, s_len, LANES), BF16),
         jax.ShapeDtypeStruct((bsz, n_kv_tiles, CARRY_ROWS, LANES), F32)],
        [pltpu.VMEM((CARRY_ROWS, LANES), F32)],
    )(x, row(kv_norm), w_kvf[:, :d].astype(BF16), w_kvf[:, d:2 * d].astype(BF16), w_f3, b_f3,
      row(jnp.tile(k_norm, N_HEADS)), red, bcast, tri, place_k, const_k)
    c_first = c_edges[:, :, 0, :N_HEADS].transpose(0, 2, 1).reshape(-1)
    c_last = c_edges[:, :, 1, :N_HEADS].transpose(0, 2, 1).reshape(-1)

    q_proj = _dense_call(
        _q_kernel, "q_proj", grid,
        [_row_tile(d), _resident((1, d)), _resident((d, d)), _resident((d, d)), _resident((1, d)),
         _row_tile(LANES), _resident((2 * d, LANES)), _resident((2 * LANES, d)),
         _resident((LANES, d)), _resident((1, d))],
        [_row_tile(2 * d), _row_tile(d)],
        [jax.ShapeDtypeStruct((bsz, s_len, 2 * d), BF16), x_shape])

    n_pairs = d // PAIR
    attention = pl.pallas_call(
        _attn_kernel, name="fox_attention",
        grid_spec=pltpu.PrefetchScalarGridSpec(
            num_scalar_prefetch=3, grid=(bsz, n_pairs),
            in_specs=[pl.BlockSpec((None, s_len, 2 * PAIR), lambda b, p, *_: (b, 0, p)),
                      pl.BlockSpec((None, s_len, 2 * PAIR), lambda b, p, *_: (b, 0, p)),
                      pl.BlockSpec((None, n_kv_tiles, PAIR, ATTN_TK), lambda b, p, *_: (b, 0, p, 0)),
                      pl.BlockSpec((None, s_len, PAIR), lambda b, p, *_: (b, 0, p))],
            out_specs=pl.BlockSpec((None, s_len, PAIR), lambda b, p, *_: (b, 0, p)),
            scratch_shapes=[pltpu.VMEM((2, ATTN_TQ, 2 * PAIR), BF16),
                            pltpu.VMEM((SCORE_SLOTS, ATTN_TK, ATTN_TQ), F32),
                            pltpu.VMEM((2, 1, ATTN_TQ), F32), pltpu.VMEM((2, PAIR, ATTN_TQ), F32)]),
        out_shape=jax.ShapeDtypeStruct((bsz, s_len, d), BF16),
        compiler_params=pltpu.CompilerParams(
            dimension_semantics=("arbitrary",) * 2, vmem_limit_bytes=VMEM_LIMIT))
    k_gain = jnp.max(jnp.abs(k_norm))

    def decay_floor(q_gain):
        qk_bound = 1.02 * HEAD_DIM ** 0.5 * jnp.max(jnp.abs(q_gain)) * k_gain
        return (-(2.0 * qk_bound + NEGLIGIBLE_LOG_WEIGHT)).reshape(1).astype(F32)

    layer_b = _dense_call(
        _layer_b_kernel, "layer_b", grid,
        [_row_tile(d), _row_tile(d), _resident((d, d))] + ffn_specs,
        _row_tile(d), x_shape, [ffn_carry])

    scale = HEAD_DIM ** -0.5 * LOG2_E
    for l in range(n_a, depth):
        j = l - n_a
        qcat, gate = q_proj(x, row(attn_norm[l]), b_w_qg[j][:, :d].astype(BF16), b_w_qg[j][:, d:].astype(BF16),
                            row(jnp.tile(q_norm[j], N_HEADS)) * scale, cp, red, bcast, place_q, const_q)
        og = attention(c_first, c_last, decay_floor(q_norm[j]), qcat, kcat, vt, gate)
        x = layer_b(x, og, b_w_out[j].astype(BF16), *ffn_operands(l))
    return x
```

```python
import functools

import jax
import jax.numpy as jnp
from jax import lax
from jax.experimental import pallas as pl
from jax.experimental.pallas import tpu as pltpu

F32 = jnp.float32
BF16 = jnp.bfloat16

EPS = 1e-6
LOG2_E = 1.4426950408889634
N_HEADS = 16
HEAD_DIM = 64
LANES = 128
PAIR = 2 * HEAD_DIM
CARRY_ROWS = 8

ROW_TILE = 512
CHUNK = 256
ATTN_TQ = 512
ATTN_TK = 512
SCORE_SLOTS = 8
NEGLIGIBLE_LOG_WEIGHT = 88.0
VMEM_LIMIT = 56 * 1024 * 1024


def _dot(a, b):
    return jnp.dot(a, b, preferred_element_type=F32)


def _rms(x, g):
    return x * lax.rsqrt(jnp.mean(x * x, axis=-1, keepdims=True) + EPS) * g


def _split2(x):
    hi = x.astype(BF16)
    lo = (x - hi.astype(F32)).astype(BF16)
    return hi, lo


def _split3(x):
    p1 = x.astype(BF16)
    r1 = x - p1.astype(F32)
    p2 = r1.astype(BF16)
    p3 = (r1 - p2.astype(F32)).astype(BF16)
    return p1, p2, p3


def _head_rms(q, red_ref, bcast_ref):
    sq_hi, sq_lo = _split2(q * q)
    ms = _dot(jnp.concatenate([sq_hi, sq_lo], axis=1), red_ref[...])
    r_hi, r_lo = _split2(lax.rsqrt(ms + EPS))
    return q * _dot(jnp.concatenate([r_hi, r_lo], axis=1), bcast_ref[...])


def _shift_down(a, first_row):
    row = lax.broadcasted_iota(jnp.int32, a.shape, 0)
    return jnp.where(row == 0, first_row, pltpu.roll(a, 1, axis=0))


def _causal_conv3(u, w, carry_ref, lo, hi):
    rows = u.shape[0]
    old = carry_ref[:, lo:hi]
    carry_ref[:, lo:hi] = u[rows - CARRY_ROWS:, :]
    u1 = _shift_down(u, old[CARRY_ROWS - 1:CARRY_ROWS, :])
    u2 = _shift_down(u1, old[CARRY_ROWS - 2:CARRY_ROWS - 1, :])
    return w[2:3, :] * u + w[1:2, :] * u1 + w[0:1, :] * u2


def _mixer(x, gn_ref, w_in_ref, cw_ref, w_out_ref, carry_ref):
    d = x.shape[1]
    xn = _rms(x, gn_ref[...]).astype(BF16)

    def project(j):
        return tuple(_dot(xn, w_in_ref[:, part * d + j * CHUNK:part * d + (j + 1) * CHUNK]) for part in range(3))

    n_chunks = d // CHUNK
    acc = jnp.zeros(x.shape, F32)
    ahead = project(0)
    for j in range(n_chunks):
        lo, hi = j * CHUNK, (j + 1) * CHUNK
        b, c, h = ahead
        if j + 1 < n_chunks:
            ahead = project(j + 1)
        u = _causal_conv3(c * h, cw_ref[:, lo:hi], carry_ref, lo, hi)
        acc = acc + _dot((b * u).astype(BF16), w_out_ref[lo:hi, :])
    return x + acc


def _conv_ffn(x, gn_ref, w_up_ref, cw_ref, w_down_ref, carry_ref):
    f = w_down_ref.shape[0]
    xn = _rms(x, gn_ref[...]).astype(BF16)

    def project(j):
        return tuple(_dot(xn, w_up_ref[:, part * f + j * CHUNK:part * f + (j + 1) * CHUNK]) for part in range(2))

    n_chunks = f // CHUNK
    acc = jnp.zeros(x.shape, F32)
    ahead = project(0)
    for j in range(n_chunks):
        lo, hi = j * CHUNK, (j + 1) * CHUNK
        a, g = ahead
        if j + 1 < n_chunks:
            ahead = project(j + 1)
        a = _causal_conv3(a, cw_ref[:, lo:hi], carry_ref, lo, hi)
        hid = a * jax.nn.sigmoid(a) * g
        acc = acc + _dot(hid.astype(BF16), w_down_ref[lo:hi, :])
    return x + acc


def _row_halves(rows):
    return [pl.ds(0, rows // 2), pl.ds(rows // 2, rows // 2)]


def _reset_at_sequence_start(*carry_refs):
    @pl.when(pl.program_id(1) == 0)
    def _():
        for ref in carry_refs:
            ref[...] = jnp.zeros(ref.shape, ref.dtype)


def _layer_a_kernel(x_ref, an_ref, w_in_ref, acw_ref, w_out_ref,
                    fn_ref, w_up_ref, fcw_ref, w_down_ref, o_ref,
                    mix_carry, ffn_carry):
    _reset_at_sequence_start(mix_carry, ffn_carry)
    halves = _row_halves(x_ref.shape[0])
    mixed = [_mixer(x_ref[rows, :], an_ref, w_in_ref, acw_ref, w_out_ref, mix_carry) for rows in halves]
    for rows, x in zip(halves, mixed):
        o_ref[rows, :] = _conv_ffn(x, fn_ref, w_up_ref, fcw_ref, w_down_ref, ffn_carry)


def _layer_b_kernel(x_ref, og_ref, w_out_ref,
                    fn_ref, w_up_ref, fcw_ref, w_down_ref, o_ref, ffn_carry):
    _reset_at_sequence_start(ffn_carry)
    halves = _row_halves(x_ref.shape[0])
    attended = [x_ref[rows, :] + _dot(og_ref[rows, :], w_out_ref[...]) for rows in halves]
    for rows, x in zip(halves, attended):
        o_ref[rows, :] = _conv_ffn(x, fn_ref, w_up_ref, fcw_ref, w_down_ref, ffn_carry)


def _interleave_pairs(main, extra, out_ref, rows):
    for p in range(main.shape[1] // PAIR):
        out_ref[rows, 2 * p * PAIR:(2 * p + 1) * PAIR] = main[:, p * PAIR:(p + 1) * PAIR].astype(out_ref.dtype)
        out_ref[rows, (2 * p + 1) * PAIR:(2 * p + 2) * PAIR] = extra[:, p * PAIR:(p + 1) * PAIR].astype(out_ref.dtype)


def _kv_kernel(x_ref, gn_ref, wk_ref, wv_ref, wf_ref, bf_ref, kg_ref,
               red_ref, bcast_ref, tri_ref, place_ref, const_ref,
               kcat_ref, vt_ref, cp_ref, edge_ref, c_carry):
    _reset_at_sequence_start(c_carry)
    halves = _row_halves(x_ref.shape[0])
    half = x_ref.shape[0] // 2
    tri = tri_ref[:half, :half]
    edges = []
    for rows in halves:
        h = _rms(x_ref[rows, :], gn_ref[...]).astype(BF16)
        vt_ref[:, rows] = _dot(h, wv_ref[...]).T.astype(vt_ref.dtype)

        f_logit = _dot(h, wf_ref[...]) + bf_ref[...]
        log_f = jnp.minimum(f_logit, 0.0) - jnp.log1p(jnp.exp(-jnp.abs(f_logit)))
        c = c_carry[0:1, :] + sum(_dot(tri, piece) for piece in _split3(log_f))
        last = c[half - 1:, :]
        c_carry[0:1, :] = last
        edges += [c[0:1, :], last]
        c1, c2, c3 = _split3(c * LOG2_E)
        lane = lax.broadcasted_iota(jnp.int32, c.shape, 1)
        cp = jnp.where(lane < N_HEADS, c1, jnp.where(lane < 2 * N_HEADS, c2, c3))
        cp_ref[rows, :] = cp

        k = _head_rms(_dot(h, wk_ref[...]), red_ref, bcast_ref) * kg_ref[...]
        extra = _dot(cp, place_ref[...]) + const_ref[...]
        _interleave_pairs(k, extra, kcat_ref, rows)
    edge_row = lax.broadcasted_iota(jnp.int32, edge_ref.shape, 0)
    edge_ref[...] = jnp.where(edge_row == 0, edges[0], jnp.where(edge_row == 1, edges[-1], 0.0))


def _q_kernel(x_ref, gn_ref, wq_ref, wg_ref, qg_ref, cp_ref,
              red_ref, bcast_ref, place_ref, const_ref,
              qcat_ref, gate_ref):
    for rows in _row_halves(x_ref.shape[0]):
        xn = _rms(x_ref[rows, :], gn_ref[...]).astype(BF16)
        gate_ref[rows, :] = jax.nn.sigmoid(_dot(xn, wg_ref[...]))
        q = _head_rms(_dot(xn, wq_ref[...]), red_ref, bcast_ref) * qg_ref[...]
        extra = _dot(cp_ref[rows, :], place_ref[...]) + const_ref[...]
        _interleave_pairs(q, extra, qcat_ref, rows)


def _attn_kernel(c_first_ref, c_last_ref, floor_ref, q_ref, k_ref, vt_ref, gate_ref, o_ref,
                 qh_scr, s_scr, m_scr, acc_scr):
    n_tiles = q_ref.shape[0] // ATTN_TQ
    pl.loop(0, n_tiles)(functools.partial(
        _attend_query_tile, c_first_ref, c_last_ref, floor_ref, q_ref, k_ref, vt_ref, gate_ref, o_ref,
        qh_scr, s_scr, m_scr, acc_scr, n_tiles))


def _attend_query_tile(c_first_ref, c_last_ref, floor_ref, q_ref, k_ref, vt_ref, gate_ref, o_ref,
                       qh_scr, s_scr, m_scr, acc_scr, n_tiles, qi):
    b, pair = pl.program_id(0), pl.program_id(1)
    decay_floor = floor_ref[0]
    q_rows = pl.ds(pl.multiple_of(qi * ATTN_TQ, ATTN_TQ), ATTN_TQ)

    def first_needed_tile(head):
        base = (b * N_HEADS + 2 * pair + head) * n_tiles
        c_query = c_first_ref[base + qi]
        return lax.fori_loop(
            0, qi, lambda j, n: n + jnp.where(c_query - c_last_ref[base + j] < decay_floor, 1, 0), 0)

    first0, first1 = first_needed_tile(0), first_needed_tile(1)
    odd = (first0 + first1) % 2
    first0, first1 = (first0 - jnp.where((odd == 1) & (first0 > 0), 1, 0),
                      first1 - jnp.where((odd == 1) & (first0 == 0), 1, 0))
    n0, n1 = qi - first0, qi - first1
    n_full = n0 + n1

    def item(u):
        in0, in1 = u < n0, u < n_full
        return jnp.where(in0, 0, jnp.where(in1, 1, 0)), jnp.where(in0, first0 + u, jnp.where(in1, first1 + u - n0, qi))

    q_t = q_ref[q_rows, :].astype(F32).T.astype(BF16)
    feature = lax.broadcasted_iota(jnp.int32, (q_t.shape[0], 1), 0)
    for head in range(2):
        qh_scr[head] = jnp.where((feature % PAIR) // HEAD_DIM == head, q_t, jnp.zeros_like(q_t))
    ones = jnp.ones((HEAD_DIM, ATTN_TK), BF16)

    def produce(head, j, slot):
        start = pl.multiple_of(j * ATTN_TK, ATTN_TK)
        s_scr[slot] = _dot(k_ref[pl.ds(start, ATTN_TK), :], qh_scr[head])

    def consume(head, j, slot, on_diagonal):
        s = s_scr[slot]
        if on_diagonal:
            key_pos = lax.broadcasted_iota(jnp.int32, s.shape, 0)
            query_pos = lax.broadcasted_iota(jnp.int32, s.shape, 1)
            s = jnp.where(key_pos <= query_pos, s, -jnp.inf)
        m = m_scr[head]
        m_new = jnp.maximum(m, jnp.max(s, axis=0, keepdims=True))
        p = jnp.exp2(s - m_new).astype(BF16)
        v_rows = pl.ds(pl.multiple_of(head * HEAD_DIM, HEAD_DIM), HEAD_DIM)
        v_aug = jnp.concatenate([vt_ref[j, v_rows, :], ones], axis=0)
        acc_scr[head] = jnp.exp2(m - m_new) * acc_scr[head] + _dot(v_aug, p)
        m_scr[head] = m_new

    def pipelined(u, n_items):
        for k in range(n_items):
            produce(*item(u + k + 1), (k + 1) % n_items)
            consume(*item(u + k), k, False)

    m_scr[...] = jnp.full(m_scr.shape, -jnp.inf, F32)
    acc_scr[...] = jnp.zeros(acc_scr.shape, F32)
    produce(*item(0), 0)

    @pl.loop(0, n_full // SCORE_SLOTS)
    def _(i):
        pipelined(SCORE_SLOTS * i, SCORE_SLOTS)

    done = n_full - n_full % SCORE_SLOTS
    run = SCORE_SLOTS // 2
    while run >= 2:
        @pl.when(n_full & run != 0)
        def _(done=done, run=run):
            pipelined(done, run)
        done = done + (n_full & run)
        run //= 2

    produce(1, qi, 1)
    consume(0, qi, 0, True)
    consume(1, qi, 1, True)

    o_t = jnp.concatenate([acc_scr[head, :HEAD_DIM, :] / acc_scr[head, HEAD_DIM:, :] for head in range(2)], axis=0)
    o_ref[q_rows, :] = (o_t.T * gate_ref[q_rows, :]).astype(o_ref.dtype)


def _resident(shape):
    return pl.BlockSpec(shape, lambda *_: (0,) * len(shape), pipeline_mode=pl.Buffered(1))


def _row_tile(cols):
    return pl.BlockSpec((None, ROW_TILE, cols), lambda b, s: (b, s, 0))


def _dense_call(kernel, name, grid, in_specs, out_specs, out_shape, scratch_shapes=()):
    return pl.pallas_call(
        kernel, name=name, grid=grid, in_specs=in_specs, out_specs=out_specs, out_shape=out_shape,
        scratch_shapes=list(scratch_shapes),
        compiler_params=pltpu.CompilerParams(
            dimension_semantics=("arbitrary", "arbitrary"), vmem_limit_bytes=VMEM_LIMIT))


def _placement_constants():
    d = N_HEADS * HEAD_DIM
    head_of_col = jnp.arange(d) // HEAD_DIM
    pos_in_head = jnp.arange(d) % HEAD_DIM
    lane = jnp.arange(LANES)
    red = (head_of_col[:, None] == lane[None, :]).astype(F32) / HEAD_DIM
    red = jnp.concatenate([red, red], axis=0).astype(BF16)
    bcast = (lane[:, None] == head_of_col[None, :]).astype(BF16)
    bcast = jnp.concatenate([bcast, bcast], axis=0)
    piece, head = lane // N_HEADS, lane % N_HEADS
    valid = (piece < 3)[:, None] & (head[:, None] == head_of_col[None, :])
    place_q = (valid & (pos_in_head[None, :] == piece[:, None])).astype(BF16)
    place_k = -(valid & (pos_in_head[None, :] == 3 + piece[:, None])).astype(BF16)
    const_q = ((pos_in_head >= 3) & (pos_in_head < 6)).astype(F32)[None, :]
    const_k = (pos_in_head < 3).astype(F32)[None, :]
    return red, bcast, place_q, const_q, place_k, const_k


def kernel(x, attn_norm, ffn_norm, a_w_in, a_conv, a_w_out, kv_norm, w_kvf, b_f, k_norm,
           b_w_qg, q_norm, b_w_out, ffn_w_up, ffn_conv, ffn_w_down):
    bsz, s_len, d = x.shape
    depth = ffn_w_up.shape[0]
    n_a = a_w_in.shape[0]
    f = ffn_w_down.shape[1]
    assert d == N_HEADS * HEAD_DIM and s_len % ROW_TILE == 0 and s_len % ATTN_TQ == 0
    assert ATTN_TQ == ATTN_TK == ROW_TILE and d % CHUNK == 0 and f % CHUNK == 0

    grid = (bsz, s_len // ROW_TILE)
    x_shape = jax.ShapeDtypeStruct((bsz, s_len, d), F32)
    red, bcast, place_q, const_q, place_k, const_k = _placement_constants()
    row = lambda v: v.reshape(1, -1)

    def ffn_operands(l):
        return (row(ffn_norm[l]), ffn_w_up[l].astype(BF16), ffn_conv[l], ffn_w_down[l].astype(BF16))

    ffn_specs = [_resident((1, d)), _resident((d, 2 * f)), _resident((3, f)), _resident((f, d))]
    ffn_carry = pltpu.VMEM((CARRY_ROWS, f), F32)

    layer_a = _dense_call(
        _layer_a_kernel, "layer_a", grid,
        [_row_tile(d), _resident((1, d)), _resident((d, 3 * d)), _resident((3, d)), _resident((d, d))] + ffn_specs,
        _row_tile(d), x_shape, [pltpu.VMEM((CARRY_ROWS, d), F32), ffn_carry])
    for l in range(n_a):
        x = layer_a(x, row(attn_norm[l]), a_w_in[l].astype(BF16), a_conv[l], a_w_out[l].astype(BF16),
                    *ffn_operands(l))

    w_f = w_kvf[:, 2 * d:]
    pad = LANES - 3 * N_HEADS
    w_f3 = jnp.pad(jnp.concatenate([w_f, w_f, w_f], axis=1), ((0, 0), (0, pad))).astype(BF16)
    b_f3 = jnp.pad(jnp.concatenate([b_f, b_f, b_f]), (0, pad)).reshape(1, LANES)
    tri = (jnp.arange(ROW_TILE)[:, None] >= jnp.arange(ROW_TILE)[None, :]).astype(BF16)
    n_kv_tiles = s_len // ATTN_TK
    kcat, vt, cp, c_edges = _dense_call(
        _kv_kernel, "kv_proj", grid,
        [_row_tile(d), _resident((1, d)), _resident((d, d)), _resident((d, d)), _resident((d, LANES)),
         _resident((1, LANES)), _resident((1, d)), _resident((2 * d, LANES)), _resident((2 * LANES, d)),
         _resident((ROW_TILE, ROW_TILE)), _resident((LANES, d)), _resident((1, d))],
        [_row_tile(2 * d), pl.BlockSpec((None, None, d, ATTN_TK), lambda b, s: (b, s, 0, 0)), _row_tile(LANES),
         pl.BlockSpec((None, None, CARRY_ROWS, LANES), lambda b, s: (b, s, 0, 0))],
        [jax.ShapeDtypeStruct((bsz, s_len, 2 * d), BF16),
         jax.ShapeDtypeStruct((bsz, n_kv_tiles, d, ATTN_TK), BF16),
         jax.ShapeDtypeStruct((bsz, s_len, LANES), BF16),
         jax.ShapeDtypeStruct((bsz, n_kv_tiles, CARRY_ROWS, LANES), F32)],
        [pltpu.VMEM((CARRY_ROWS, LANES), F32)],
    )(x, row(kv_norm), w_kvf[:, :d].astype(BF16), w_kvf[:, d:2 * d].astype(BF16), w_f3, b_f3,
      row(jnp.tile(k_norm, N_HEADS)), red, bcast, tri, place_k, const_k)
    c_first = c_edges[:, :, 0, :N_HEADS].transpose(0, 2, 1).reshape(-1)
    c_last = c_edges[:, :, 1, :N_HEADS].transpose(0, 2, 1).reshape(-1)

    q_proj = _dense_call(
        _q_kernel, "q_proj", grid,
        [_row_tile(d), _resident((1, d)), _resident((d, d)), _resident((d, d)), _resident((1, d)),
         _row_tile(LANES), _resident((2 * d, LANES)), _resident((2 * LANES, d)),
         _resident((LANES, d)), _resident((1, d))],
        [_row_tile(2 * d), _row_tile(d)],
        [jax.ShapeDtypeStruct((bsz, s_len, 2 * d), BF16), x_shape])

    n_pairs = d // PAIR
    attention = pl.pallas_call(
        _attn_kernel, name="fox_attention",
        grid_spec=pltpu.PrefetchScalarGridSpec(
            num_scalar_prefetch=3, grid=(bsz, n_pairs),
            in_specs=[pl.BlockSpec((None, s_len, 2 * PAIR), lambda b, p, *_: (b, 0, p)),
                      pl.BlockSpec((None, s_len, 2 * PAIR), lambda b, p, *_: (b, 0, p)),
                      pl.BlockSpec((None, n_kv_tiles, PAIR, ATTN_TK), lambda b, p, *_: (b, 0, p, 0)),
                      pl.BlockSpec((None, s_len, PAIR), lambda b, p, *_: (b, 0, p))],
            out_specs=pl.BlockSpec((None, s_len, PAIR), lambda b, p, *_: (b, 0, p)),
            scratch_shapes=[pltpu.VMEM((2, 2 * PAIR, ATTN_TQ), BF16),
                            pltpu.VMEM((SCORE_SLOTS, ATTN_TK, ATTN_TQ), F32),
                            pltpu.VMEM((2, 1, ATTN_TQ), F32), pltpu.VMEM((2, PAIR, ATTN_TQ), F32)]),
        out_shape=jax.ShapeDtypeStruct((bsz, s_len, d), BF16),
        compiler_params=pltpu.CompilerParams(
            dimension_semantics=("arbitrary",) * 2, vmem_limit_bytes=VMEM_LIMIT))
    k_gain = jnp.max(jnp.abs(k_norm))

    def decay_floor(q_gain):
        qk_bound = 1.02 * HEAD_DIM ** 0.5 * jnp.max(jnp.abs(q_gain)) * k_gain
        return (-(2.0 * qk_bound + NEGLIGIBLE_LOG_WEIGHT)).reshape(1).astype(F32)

    layer_b = _dense_call(
        _layer_b_kernel, "layer_b", grid,
        [_row_tile(d), _row_tile(d), _resident((d, d))] + ffn_specs,
        _row_tile(d), x_shape, [ffn_carry])

    scale = HEAD_DIM ** -0.5 * LOG2_E
    for l in range(n_a, depth):
        j = l - n_a
        qcat, gate = q_proj(x, row(attn_norm[l]), b_w_qg[j][:, :d].astype(BF16), b_w_qg[j][:, d:].astype(BF16),
                            row(jnp.tile(q_norm[j], N_HEADS)) * scale, cp, red, bcast, place_q, const_q)
        og = attention(c_first, c_last, decay_floor(q_norm[j]), qcat, kcat, vt, gate)
        x = layer_b(x, og, b_w_out[j].astype(BF16), *ffn_operands(l))
    return x
```

```python
import functools

import jax
import jax.numpy as jnp
from jax import lax
from jax.experimental import pallas as pl
from jax.experimental.pallas import tpu as pltpu

F32 = jnp.float32
BF16 = jnp.bfloat16

EPS = 1e-6
LOG2_E = 1.4426950408889634
N_HEADS = 16
HEAD_DIM = 64
LANES = 128
PAIR = 2 * HEAD_DIM
CARRY_ROWS = 8

ROW_TILE = 512
CHUNK = 256
ATTN_TQ = 512
ATTN_TK = 512
SCORE_SLOTS = 8
NEGLIGIBLE_LOG_WEIGHT = 88.0
VMEM_LIMIT = 56 * 1024 * 1024


def _dot(a, b):
    return jnp.dot(a, b, preferred_element_type=F32)


def _rms(x, g):
    return x * lax.rsqrt(jnp.mean(x * x, axis=-1, keepdims=True) + EPS) * g


def _split2(x):
    hi = x.astype(BF16)
    lo = (x - hi.astype(F32)).astype(BF16)
    return hi, lo


def _split3(x):
    p1 = x.astype(BF16)
    r1 = x - p1.astype(F32)
    p2 = r1.astype(BF16)
    p3 = (r1 - p2.astype(F32)).astype(BF16)
    return p1, p2, p3


def _head_rms(q, red_ref, bcast_ref):
    sq_hi, sq_lo = _split2(q * q)
    ms = _dot(jnp.concatenate([sq_hi, sq_lo], axis=1), red_ref[...])
    r_hi, r_lo = _split2(lax.rsqrt(ms + EPS))
    return q * _dot(jnp.concatenate([r_hi, r_lo], axis=1), bcast_ref[...])


def _shift_down(a, first_row):
    row = lax.broadcasted_iota(jnp.int32, a.shape, 0)
    return jnp.where(row == 0, first_row, pltpu.roll(a, 1, axis=0))


def _causal_conv3(u, w, carry_ref, lo, hi):
    rows = u.shape[0]
    old = carry_ref[:, lo:hi]
    carry_ref[:, lo:hi] = u[rows - CARRY_ROWS:, :]
    u1 = _shift_down(u, old[CARRY_ROWS - 1:CARRY_ROWS, :])
    u2 = _shift_down(u1, old[CARRY_ROWS - 2:CARRY_ROWS - 1, :])
    return w[2:3, :] * u + w[1:2, :] * u1 + w[0:1, :] * u2


def _mixer(x, gn_ref, w_in_ref, cw_ref, w_out_ref, carry_ref):
    d = x.shape[1]
    xn = _rms(x, gn_ref[...]).astype(BF16)

    def project(j):
        return tuple(_dot(xn, w_in_ref[:, part * d + j * CHUNK:part * d + (j + 1) * CHUNK]) for part in range(3))

    n_chunks = d // CHUNK
    acc = jnp.zeros(x.shape, F32)
    ahead = project(0)
    for j in range(n_chunks):
        lo, hi = j * CHUNK, (j + 1) * CHUNK
        b, c, h = ahead
        if j + 1 < n_chunks:
            ahead = project(j + 1)
        u = _causal_conv3(c * h, cw_ref[:, lo:hi], carry_ref, lo, hi)
        acc = acc + _dot((b * u).astype(BF16), w_out_ref[lo:hi, :])
    return x + acc


def _conv_ffn(x, gn_ref, w_up_ref, cw_ref, w_down_ref, carry_ref):
    f = w_down_ref.shape[0]
    xn = _rms(x, gn_ref[...]).astype(BF16)

    def project(j):
        return tuple(_dot(xn, w_up_ref[:, part * f + j * CHUNK:part * f + (j + 1) * CHUNK]) for part in range(2))

    n_chunks = f // CHUNK
    acc = jnp.zeros(x.shape, F32)
    ahead = project(0)
    for j in range(n_chunks):
        lo, hi = j * CHUNK, (j + 1) * CHUNK
        a, g = ahead
        if j + 1 < n_chunks:
            ahead = project(j + 1)
        a = _causal_conv3(a, cw_ref[:, lo:hi], carry_ref, lo, hi)
        hid = a * jax.nn.sigmoid(a) * g
        acc = acc + _dot(hid.astype(BF16), w_down_ref[lo:hi, :])
    return x + acc


def _row_halves(rows):
    return [pl.ds(0, rows // 2), pl.ds(rows // 2, rows // 2)]


def _reset_at_sequence_start(*carry_refs):
    @pl.when(pl.program_id(1) == 0)
    def _():
        for ref in carry_refs:
            ref[...] = jnp.zeros(ref.shape, ref.dtype)


def _layer_a_kernel(x_ref, an_ref, w_in_ref, acw_ref, w_out_ref,
                    fn_ref, w_up_ref, fcw_ref, w_down_ref, o_ref,
                    mix_carry, ffn_carry):
    _reset_at_sequence_start(mix_carry, ffn_carry)
    halves = _row_halves(x_ref.shape[0])
    mixed = [_mixer(x_ref[rows, :], an_ref, w_in_ref, acw_ref, w_out_ref, mix_carry) for rows in halves]
    for rows, x in zip(halves, mixed):
        o_ref[rows, :] = _conv_ffn(x, fn_ref, w_up_ref, fcw_ref, w_down_ref, ffn_carry)


def _layer_b_kernel(x_ref, og_ref, w_out_ref,
                    fn_ref, w_up_ref, fcw_ref, w_down_ref, o_ref, ffn_carry):
    _reset_at_sequence_start(ffn_carry)
    halves = _row_halves(x_ref.shape[0])
    attended = [x_ref[rows, :] + _dot(og_ref[rows, :], w_out_ref[...]) for rows in halves]
    for rows, x in zip(halves, attended):
        o_ref[rows, :] = _conv_ffn(x, fn_ref, w_up_ref, fcw_ref, w_down_ref, ffn_carry)


def _interleave_pairs(main, extra, out_ref, rows):
    for p in range(main.shape[1] // PAIR):
        out_ref[rows, 2 * p * PAIR:(2 * p + 1) * PAIR] = main[:, p * PAIR:(p + 1) * PAIR].astype(out_ref.dtype)
        out_ref[rows, (2 * p + 1) * PAIR:(2 * p + 2) * PAIR] = extra[:, p * PAIR:(p + 1) * PAIR].astype(out_ref.dtype)


def _kv_kernel(x_ref, gn_ref, wk_ref, wv_ref, wf_ref, bf_ref, kg_ref,
               red_ref, bcast_ref, tri_ref, place_ref, const_ref,
               kcat_ref, vt_ref, cp_ref, edge_ref, c_carry):
    _reset_at_sequence_start(c_carry)
    halves = _row_halves(x_ref.shape[0])
    half = x_ref.shape[0] // 2
    tri = tri_ref[:half, :half]
    edges = []
    for rows in halves:
        h = _rms(x_ref[rows, :], gn_ref[...]).astype(BF16)
        vt_ref[:, rows] = _dot(h, wv_ref[...]).T.astype(vt_ref.dtype)

        f_logit = _dot(h, wf_ref[...]) + bf_ref[...]
        log_f = jnp.minimum(f_logit, 0.0) - jnp.log1p(jnp.exp(-jnp.abs(f_logit)))
        c = c_carry[0:1, :] + sum(_dot(tri, piece) for piece in _split3(log_f))
        last = c[half - 1:, :]
        c_carry[0:1, :] = last
        edges += [c[0:1, :], last]
        c1, c2, c3 = _split3(c * LOG2_E)
        lane = lax.broadcasted_iota(jnp.int32, c.shape, 1)
        cp = jnp.where(lane < N_HEADS, c1, jnp.where(lane < 2 * N_HEADS, c2, c3))
        cp_ref[rows, :] = cp

        k = _head_rms(_dot(h, wk_ref[...]), red_ref, bcast_ref) * kg_ref[...]
        extra = _dot(cp, place_ref[...]) + const_ref[...]
        _interleave_pairs(k, extra, kcat_ref, rows)
    edge_row = lax.broadcasted_iota(jnp.int32, edge_ref.shape, 0)
    edge_ref[...] = jnp.where(edge_row == 0, edges[0], jnp.where(edge_row == 1, edges[-1], 0.0))


def _q_kernel(x_ref, gn_ref, wq_ref, wg_ref, qg_ref, cp_ref,
              red_ref, bcast_ref, place_ref, const_ref,
              qcat_ref, gate_ref):
    for rows in _row_halves(x_ref.shape[0]):
        xn = _rms(x_ref[rows, :], gn_ref[...]).astype(BF16)
        gate_ref[rows, :] = jax.nn.sigmoid(_dot(xn, wg_ref[...]))
        q = _head_rms(_dot(xn, wq_ref[...]), red_ref, bcast_ref) * qg_ref[...]
        extra = _dot(cp_ref[rows, :], place_ref[...]) + const_ref[...]
        _interleave_pairs(q, extra, qcat_ref, rows)


def _attn_kernel(c_first_ref, c_last_ref, floor_ref, q_ref, k_ref, vt_ref, gate_ref, o_ref,
                 qh_scr, s_scr, m_scr, acc_scr):
    n_tiles = q_ref.shape[0] // ATTN_TQ
    _stage_queries(q_ref, qh_scr, 0, 0)
    _produce_scores(k_ref, qh_scr, s_scr, 0, 0, 0, 0)
    pl.loop(0, n_tiles)(functools.partial(
        _attend_query_tile, c_first_ref, c_last_ref, floor_ref, q_ref, k_ref, vt_ref, gate_ref, o_ref,
        qh_scr, s_scr, m_scr, acc_scr, n_tiles, pl.program_id(0), pl.program_id(1)))


def _stage_queries(q_ref, qh_scr, tile, buf):
    rows = pl.ds(pl.multiple_of(tile * ATTN_TQ, ATTN_TQ), ATTN_TQ)
    q_t = q_ref[rows, :].astype(F32).T.astype(BF16)
    feature = lax.broadcasted_iota(jnp.int32, (q_t.shape[0], 1), 0)
    for head in range(2):
        qh_scr[2 * buf + head] = jnp.where((feature % PAIR) // HEAD_DIM == head, q_t, jnp.zeros_like(q_t))


def _produce_scores(k_ref, qh_scr, s_scr, head, j, slot, buf):
    start = pl.multiple_of(j * ATTN_TK, ATTN_TK)
    s_scr[slot] = _dot(k_ref[pl.ds(start, ATTN_TK), :], qh_scr[2 * buf + head])


def _attend_query_tile(c_first_ref, c_last_ref, floor_ref, q_ref, k_ref, vt_ref, gate_ref, o_ref,
                       qh_scr, s_scr, m_scr, acc_scr, n_tiles, b, pair, qi):
    decay_floor = floor_ref[0]
    q_rows = pl.ds(pl.multiple_of(qi * ATTN_TQ, ATTN_TQ), ATTN_TQ)
    next_qi = jnp.minimum(qi + 1, n_tiles - 1)
    buf = qi % 2

    def first_needed_tiles(tile):
        def one_head(head):
            base = (b * N_HEADS + 2 * pair + head) * n_tiles
            c_query = c_first_ref[base + tile]
            return lax.fori_loop(
                0, tile, lambda j, n: n + jnp.where(c_query - c_last_ref[base + j] < decay_floor, 1, 0), 0)

        f0, f1 = one_head(0), one_head(1)
        odd = (f0 + f1) % 2
        return f0 - jnp.where((odd == 1) & (f0 > 0), 1, 0), f1 - jnp.where((odd == 1) & (f0 == 0), 1, 0)

    def item_of(u, tile, f0, f1):
        n0 = tile - f0
        in0, in1 = u < n0, u < n0 + tile - f1
        return jnp.where(in0, 0, jnp.where(in1, 1, 0)), jnp.where(in0, f0 + u, jnp.where(in1, f1 + u - n0, tile))

    first0, first1 = first_needed_tiles(qi)
    next_firsts = first_needed_tiles(next_qi)
    n_full = 2 * qi - first0 - first1

    def item(u):
        return item_of(u, qi, first0, first1)

    ones = jnp.ones((HEAD_DIM, ATTN_TK), BF16)

    def produce(head, j, slot):
        _produce_scores(k_ref, qh_scr, s_scr, head, j, slot, buf)

    def consume(head, j, slot, on_diagonal):
        s = s_scr[slot]
        if on_diagonal:
            key_pos = lax.broadcasted_iota(jnp.int32, s.shape, 0)
            query_pos = lax.broadcasted_iota(jnp.int32, s.shape, 1)
            s = jnp.where(key_pos <= query_pos, s, -jnp.inf)
        m = m_scr[head]
        m_new = jnp.maximum(m, jnp.max(s, axis=0, keepdims=True))
        p = jnp.exp2(s - m_new).astype(BF16)
        v_rows = pl.ds(pl.multiple_of(head * HEAD_DIM, HEAD_DIM), HEAD_DIM)
        v_aug = jnp.concatenate([vt_ref[j, v_rows, :], ones], axis=0)
        acc_scr[head] = jnp.exp2(m - m_new) * acc_scr[head] + _dot(v_aug, p)
        m_scr[head] = m_new

    def pipelined(u, n_items):
        for k in range(n_items):
            produce(*item(u + k + 1), (k + 1) % n_items)
            consume(*item(u + k), k, False)

    m_scr[...] = jnp.full(m_scr.shape, -jnp.inf, F32)
    acc_scr[...] = jnp.zeros(acc_scr.shape, F32)

    @pl.loop(0, n_full // SCORE_SLOTS)
    def _(i):
        pipelined(SCORE_SLOTS * i, SCORE_SLOTS)

    done = n_full - n_full % SCORE_SLOTS
    run = SCORE_SLOTS // 2
    while run >= 2:
        @pl.when(n_full & run != 0)
        def _(done=done, run=run):
            pipelined(done, run)
        done = done + (n_full & run)
        run //= 2

    produce(1, qi, 1)
    consume(0, qi, 0, True)
    _stage_queries(q_ref, qh_scr, next_qi, 1 - buf)
    _produce_scores(k_ref, qh_scr, s_scr, *item_of(0, next_qi, *next_firsts), 0, 1 - buf)
    consume(1, qi, 1, True)

    o_t = jnp.concatenate([acc_scr[head, :HEAD_DIM, :] / acc_scr[head, HEAD_DIM:, :] for head in range(2)], axis=0)
    o_ref[q_rows, :] = (o_t.T * gate_ref[q_rows, :]).astype(o_ref.dtype)


def _resident(shape):
    return pl.BlockSpec(shape, lambda *_: (0,) * len(shape), pipeline_mode=pl.Buffered(1))


def _row_tile(cols):
    return pl.BlockSpec((None, ROW_TILE, cols), lambda b, s: (b, s, 0))


def _dense_call(kernel, name, grid, in_specs, out_specs, out_shape, scratch_shapes=()):
    return pl.pallas_call(
        kernel, name=name, grid=grid, in_specs=in_specs, out_specs=out_specs, out_shape=out_shape,
        scratch_shapes=list(scratch_shapes),
        compiler_params=pltpu.CompilerParams(
            dimension_semantics=("arbitrary", "arbitrary"), vmem_limit_bytes=VMEM_LIMIT))


def _placement_constants():
    d = N_HEADS * HEAD_DIM
    head_of_col = jnp.arange(d) // HEAD_DIM
    pos_in_head = jnp.arange(d) % HEAD_DIM
    lane = jnp.arange(LANES)
    red = (head_of_col[:, None] == lane[None, :]).astype(F32) / HEAD_DIM
    red = jnp.concatenate([red, red], axis=0).astype(BF16)
    bcast = (lane[:, None] == head_of_col[None, :]).astype(BF16)
    bcast = jnp.concatenate([bcast, bcast], axis=0)
    piece, head = lane // N_HEADS, lane % N_HEADS
    valid = (piece < 3)[:, None] & (head[:, None] == head_of_col[None, :])
    place_q = (valid & (pos_in_head[None, :] == piece[:, None])).astype(BF16)
    place_k = -(valid & (pos_in_head[None, :] == 3 + piece[:, None])).astype(BF16)
    const_q = ((pos_in_head >= 3) & (pos_in_head < 6)).astype(F32)[None, :]
    const_k = (pos_in_head < 3).astype(F32)[None, :]
    return red, bcast, place_q, const_q, place_k, const_k


def kernel(x, attn_norm, ffn_norm, a_w_in, a_conv, a_w_out, kv_norm, w_kvf, b_f, k_norm,
           b_w_qg, q_norm, b_w_out, ffn_w_up, ffn_conv, ffn_w_down):
    bsz, s_len, d = x.shape
    depth = ffn_w_up.shape[0]
    n_a = a_w_in.shape[0]
    f = ffn_w_down.shape[1]
    assert d == N_HEADS * HEAD_DIM and s_len % ROW_TILE == 0 and s_len % ATTN_TQ == 0
    assert ATTN_TQ == ATTN_TK == ROW_TILE and d % CHUNK == 0 and f % CHUNK == 0

    grid = (bsz, s_len // ROW_TILE)
    x_shape = jax.ShapeDtypeStruct((bsz, s_len, d), F32)
    red, bcast, place_q, const_q, place_k, const_k = _placement_constants()
    row = lambda v: v.reshape(1, -1)

    def ffn_operands(l):
        return (row(ffn_norm[l]), ffn_w_up[l].astype(BF16), ffn_conv[l], ffn_w_down[l].astype(BF16))

    ffn_specs = [_resident((1, d)), _resident((d, 2 * f)), _resident((3, f)), _resident((f, d))]
    ffn_carry = pltpu.VMEM((CARRY_ROWS, f), F32)

    layer_a = _dense_call(
        _layer_a_kernel, "layer_a", grid,
        [_row_tile(d), _resident((1, d)), _resident((d, 3 * d)), _resident((3, d)), _resident((d, d))] + ffn_specs,
        _row_tile(d), x_shape, [pltpu.VMEM((CARRY_ROWS, d), F32), ffn_carry])
    for l in range(n_a):
        x = layer_a(x, row(attn_norm[l]), a_w_in[l].astype(BF16), a_conv[l], a_w_out[l].astype(BF16),
                    *ffn_operands(l))

    w_f = w_kvf[:, 2 * d:]
    pad = LANES - 3 * N_HEADS
    w_f3 = jnp.pad(jnp.concatenate([w_f, w_f, w_f], axis=1), ((0, 0), (0, pad))).astype(BF16)
    b_f3 = jnp.pad(jnp.concatenate([b_f, b_f, b_f]), (0, pad)).reshape(1, LANES)
    tri = (jnp.arange(ROW_TILE)[:, None] >= jnp.arange(ROW_TILE)[None, :]).astype(BF16)
    n_kv_tiles = s_len // ATTN_TK
    kcat, vt, cp, c_edges = _dense_call(
        _kv_kernel, "kv_proj", grid,
        [_row_tile(d), _resident((1, d)), _resident((d, d)), _resident((d, d)), _resident((d, LANES)),
         _resident((1, LANES)), _resident((1, d)), _resident((2 * d, LANES)), _resident((2 * LANES, d)),
         _resident((ROW_TILE, ROW_TILE)), _resident((LANES, d)), _resident((1, d))],
        [_row_tile(2 * d), pl.BlockSpec((None, None, d, ATTN_TK), lambda b, s: (b, s, 0, 0)), _row_tile(LANES),
         pl.BlockSpec((None, None, CARRY_ROWS, LANES), lambda b, s: (b, s, 0, 0))],
        [jax.ShapeDtypeStruct((bsz, s_len, 2 * d), BF16),
         jax.ShapeDtypeStruct((bsz, n_kv_tiles, d, ATTN_TK), BF16),
         jax.ShapeDtypeStruct((bsz, s_len, LANES), BF16),
         jax.ShapeDtypeStruct((bsz, n_kv_tiles, CARRY_ROWS, LANES), F32)],
        [pltpu.VMEM((CARRY_ROWS, LANES), F32)],
    )(x, row(kv_norm), w_kvf[:, :d].astype(BF16), w_kvf[:, d:2 * d].astype(BF16), w_f3, b_f3,
      row(jnp.tile(k_norm, N_HEADS)), red, bcast, tri, place_k, const_k)
    c_first = c_edges[:, :, 0, :N_HEADS].transpose(0, 2, 1).reshape(-1)
    c_last = c_edges[:, :, 1, :N_HEADS].transpose(0, 2, 1).reshape(-1)

    q_proj = _dense_call(
        _q_kernel, "q_proj", grid,
        [_row_tile(d), _resident((1, d)), _resident((d, d)), _resident((d, d)), _resident((1, d)),
         _row_tile(LANES), _resident((2 * d, LANES)), _resident((2 * LANES, d)),
         _resident((LANES, d)), _resident((1, d))],
        [_row_tile(2 * d), _row_tile(d)],
        [jax.ShapeDtypeStruct((bsz, s_len, 2 * d), BF16), x_shape])

    n_pairs = d // PAIR
    attention = pl.pallas_call(
        _attn_kernel, name="fox_attention",
        grid_spec=pltpu.PrefetchScalarGridSpec(
            num_scalar_prefetch=3, grid=(bsz, n_pairs),
            in_specs=[pl.BlockSpec((None, s_len, 2 * PAIR), lambda b, p, *_: (b, 0, p)),
                      pl.BlockSpec((None, s_len, 2 * PAIR), lambda b, p, *_: (b, 0, p)),
                      pl.BlockSpec((None, n_kv_tiles, PAIR, ATTN_TK), lambda b, p, *_: (b, 0, p, 0)),
                      pl.BlockSpec((None, s_len, PAIR), lambda b, p, *_: (b, 0, p))],
            out_specs=pl.BlockSpec((None, s_len, PAIR), lambda b, p, *_: (b, 0, p)),
            scratch_shapes=[pltpu.VMEM((4, 2 * PAIR, ATTN_TQ), BF16),
                            pltpu.VMEM((SCORE_SLOTS, ATTN_TK, ATTN_TQ), F32),
                            pltpu.VMEM((2, 1, ATTN_TQ), F32), pltpu.VMEM((2, PAIR, ATTN_TQ), F32)]),
        out_shape=jax.ShapeDtypeStruct((bsz, s_len, d), BF16),
        compiler_params=pltpu.CompilerParams(
            dimension_semantics=("arbitrary",) * 2, vmem_limit_bytes=VMEM_LIMIT))
    k_gain = jnp.max(jnp.abs(k_norm))

    def decay_floor(q_gain):
        qk_bound = 1.02 * HEAD_DIM ** 0.5 * jnp.max(jnp.abs(q_gain)) * k_gain
        return (-(2.0 * qk_bound + NEGLIGIBLE_LOG_WEIGHT)).reshape(1).astype(F32)

    layer_b = _dense_call(
        _layer_b_kernel, "layer_b", grid,
        [_row_tile(d), _row_tile(d), _resident((d, d))] + ffn_specs,
        _row_tile(d), x_shape, [ffn_carry])

    scale = HEAD_DIM ** -0.5 * LOG2_E
    for l in range(n_a, depth):
        j = l - n_a
        qcat, gate = q_proj(x, row(attn_norm[l]), b_w_qg[j][:, :d].astype(BF16), b_w_qg[j][:, d:].astype(BF16),
                            row(jnp.tile(q_norm[j], N_HEADS)) * scale, cp, red, bcast, place_q, const_q)
        og = attention(c_first, c_last, decay_floor(q_norm[j]), qcat, kcat, vt, gate)
        x = layer_b(x, og, b_w_out[j].astype(BF16), *ffn_operands(l))
    return x
```

```python
import functools

import jax
import jax.numpy as jnp
from jax import lax
from jax.experimental import pallas as pl
from jax.experimental.pallas import tpu as pltpu

F32 = jnp.float32
BF16 = jnp.bfloat16

EPS = 1e-6
LOG2_E = 1.4426950408889634
N_HEADS = 16
HEAD_DIM = 64
LANES = 128
PAIR = 2 * HEAD_DIM
CARRY_ROWS = 8

ROW_TILE = 512
CHUNK = 256
ATTN_TQ = 512
ATTN_TK = 512
SCORE_SLOTS = 8
NEGLIGIBLE_LOG_WEIGHT = 88.0
VMEM_LIMIT = 56 * 1024 * 1024


def _dot(a, b):
    return jnp.dot(a, b, preferred_element_type=F32)


def _rms(x, g):
    return x * lax.rsqrt(jnp.mean(x * x, axis=-1, keepdims=True) + EPS) * g


def _split2(x):
    hi = x.astype(BF16)
    lo = (x - hi.astype(F32)).astype(BF16)
    return hi, lo


def _split3(x):
    p1 = x.astype(BF16)
    r1 = x - p1.astype(F32)
    p2 = r1.astype(BF16)
    p3 = (r1 - p2.astype(F32)).astype(BF16)
    return p1, p2, p3


def _head_rms(q, red_ref, bcast_ref):
    sq_hi, sq_lo = _split2(q * q)
    ms = _dot(jnp.concatenate([sq_hi, sq_lo], axis=1), red_ref[...])
    r_hi, r_lo = _split2(lax.rsqrt(ms + EPS))
    return q * _dot(jnp.concatenate([r_hi, r_lo], axis=1), bcast_ref[...])


def _shift_down(a, first_row):
    row = lax.broadcasted_iota(jnp.int32, a.shape, 0)
    return jnp.where(row == 0, first_row, pltpu.roll(a, 1, axis=0))


def _causal_conv3(u, w, carry_ref, lo, hi):
    rows = u.shape[0]
    old = carry_ref[:, lo:hi]
    carry_ref[:, lo:hi] = u[rows - CARRY_ROWS:, :]
    u1 = _shift_down(u, old[CARRY_ROWS - 1:CARRY_ROWS, :])
    u2 = _shift_down(u1, old[CARRY_ROWS - 2:CARRY_ROWS - 1, :])
    return w[2:3, :] * u + w[1:2, :] * u1 + w[0:1, :] * u2


def _mixer(x, gn_ref, w_in_ref, cw_ref, w_out_ref, carry_ref):
    d = x.shape[1]
    xn = _rms(x, gn_ref[...]).astype(BF16)

    def project(j):
        return tuple(_dot(xn, w_in_ref[:, part * d + j * CHUNK:part * d + (j + 1) * CHUNK]) for part in range(3))

    n_chunks = d // CHUNK
    acc = jnp.zeros(x.shape, F32)
    ahead = project(0)
    for j in range(n_chunks):
        lo, hi = j * CHUNK, (j + 1) * CHUNK
        b, c, h = ahead
        if j + 1 < n_chunks:
            ahead = project(j + 1)
        u = _causal_conv3(c * h, cw_ref[:, lo:hi], carry_ref, lo, hi)
        acc = acc + _dot((b * u).astype(BF16), w_out_ref[lo:hi, :])
    return x + acc


def _conv_ffn(x, gn_ref, w_up_ref, cw_ref, w_down_ref, carry_ref):
    f = w_down_ref.shape[0]
    xn = _rms(x, gn_ref[...]).astype(BF16)

    def project(j):
        return tuple(_dot(xn, w_up_ref[:, part * f + j * CHUNK:part * f + (j + 1) * CHUNK]) for part in range(2))

    n_chunks = f // CHUNK
    acc = jnp.zeros(x.shape, F32)
    ahead = project(0)
    for j in range(n_chunks):
        lo, hi = j * CHUNK, (j + 1) * CHUNK
        a, g = ahead
        if j + 1 < n_chunks:
            ahead = project(j + 1)
        a = _causal_conv3(a, cw_ref[:, lo:hi], carry_ref, lo, hi)
        hid = a * jax.nn.sigmoid(a) * g
        acc = acc + _dot(hid.astype(BF16), w_down_ref[lo:hi, :])
    return x + acc


def _row_halves(rows):
    return [pl.ds(0, rows // 2), pl.ds(rows // 2, rows // 2)]


def _reset_at_sequence_start(*carry_refs):
    @pl.when(pl.program_id(1) == 0)
    def _():
        for ref in carry_refs:
            ref[...] = jnp.zeros(ref.shape, ref.dtype)


def _layer_a_kernel(x_ref, an_ref, w_in_ref, acw_ref, w_out_ref,
                    fn_ref, w_up_ref, fcw_ref, w_down_ref, o_ref,
                    mix_carry, ffn_carry):
    _reset_at_sequence_start(mix_carry, ffn_carry)
    halves = _row_halves(x_ref.shape[0])
    mixed = [_mixer(x_ref[rows, :], an_ref, w_in_ref, acw_ref, w_out_ref, mix_carry) for rows in halves]
    for rows, x in zip(halves, mixed):
        o_ref[rows, :] = _conv_ffn(x, fn_ref, w_up_ref, fcw_ref, w_down_ref, ffn_carry)


def _layer_b_kernel(x_ref, og_ref, w_out_ref,
                    fn_ref, w_up_ref, fcw_ref, w_down_ref, o_ref, ffn_carry):
    _reset_at_sequence_start(ffn_carry)
    halves = _row_halves(x_ref.shape[0])
    attended = [x_ref[rows, :] + _dot(og_ref[rows, :], w_out_ref[...]) for rows in halves]
    for rows, x in zip(halves, attended):
        o_ref[rows, :] = _conv_ffn(x, fn_ref, w_up_ref, fcw_ref, w_down_ref, ffn_carry)


def _interleave_pairs(main, extra, out_ref, rows):
    for p in range(main.shape[1] // PAIR):
        out_ref[rows, 2 * p * PAIR:(2 * p + 1) * PAIR] = main[:, p * PAIR:(p + 1) * PAIR].astype(out_ref.dtype)
        out_ref[rows, (2 * p + 1) * PAIR:(2 * p + 2) * PAIR] = extra[:, p * PAIR:(p + 1) * PAIR].astype(out_ref.dtype)


def _kv_kernel(x_ref, gn_ref, wk_ref, wv_ref, wf_ref, bf_ref, kg_ref,
               red_ref, bcast_ref, tri_ref, place_ref, const_ref,
               kcat_ref, vt_ref, cp_ref, edge_ref, c_carry):
    _reset_at_sequence_start(c_carry)
    halves = _row_halves(x_ref.shape[0])
    half = x_ref.shape[0] // 2
    tri = tri_ref[:half, :half]
    edges = []
    for rows in halves:
        h = _rms(x_ref[rows, :], gn_ref[...]).astype(BF16)
        vt_ref[:, rows] = _dot(h, wv_ref[...]).T.astype(vt_ref.dtype)

        f_logit = _dot(h, wf_ref[...]) + bf_ref[...]
        log_f = jnp.minimum(f_logit, 0.0) - jnp.log1p(jnp.exp(-jnp.abs(f_logit)))
        c = c_carry[0:1, :] + sum(_dot(tri, piece) for piece in _split3(log_f))
        last = c[half - 1:, :]
        c_carry[0:1, :] = last
        edges += [c[0:1, :], last]
        c1, c2, c3 = _split3(c * LOG2_E)
        lane = lax.broadcasted_iota(jnp.int32, c.shape, 1)
        cp = jnp.where(lane < N_HEADS, c1, jnp.where(lane < 2 * N_HEADS, c2, c3))
        cp_ref[rows, :] = cp

        k = _head_rms(_dot(h, wk_ref[...]), red_ref, bcast_ref) * kg_ref[...]
        extra = _dot(cp, place_ref[...]) + const_ref[...]
        _interleave_pairs(k, extra, kcat_ref, rows)
    edge_row = lax.broadcasted_iota(jnp.int32, edge_ref.shape, 0)
    edge_ref[...] = jnp.where(edge_row == 0, edges[0], jnp.where(edge_row == 1, edges[-1], 0.0))


def _q_kernel(x_ref, gn_ref, wq_ref, wg_ref, qg_ref, cp_ref,
              red_ref, bcast_ref, place_ref, const_ref,
              qcat_ref, gate_ref):
    for rows in _row_halves(x_ref.shape[0]):
        xn = _rms(x_ref[rows, :], gn_ref[...]).astype(BF16)
        gate_ref[rows, :] = jax.nn.sigmoid(_dot(xn, wg_ref[...]))
        q = _head_rms(_dot(xn, wq_ref[...]), red_ref, bcast_ref) * qg_ref[...]
        extra = _dot(cp_ref[rows, :], place_ref[...]) + const_ref[...]
        _interleave_pairs(q, extra, qcat_ref, rows)


def _attn_kernel(c_first_ref, c_last_ref, floor_ref, q_ref, k_ref, vt_ref, gate_ref, o_ref,
                 qh_scr, s_scr, m_scr, acc_scr, plan_scr):
    n_tiles = q_ref.shape[0] // ATTN_TQ
    plan_scr[0] = 0
    plan_scr[1] = 0
    _stage_queries(q_ref, qh_scr, 0, 0)
    _produce_scores(k_ref, qh_scr, s_scr, 0, 0, 0, 0)
    pl.loop(0, n_tiles)(functools.partial(
        _attend_query_tile, c_first_ref, c_last_ref, floor_ref, q_ref, k_ref, vt_ref, gate_ref, o_ref,
        qh_scr, s_scr, m_scr, acc_scr, plan_scr, n_tiles, pl.program_id(0), pl.program_id(1)))


def _stage_queries(q_ref, qh_scr, tile, buf):
    rows = pl.ds(pl.multiple_of(tile * ATTN_TQ, ATTN_TQ), ATTN_TQ)
    q_t = q_ref[rows, :].astype(F32).T.astype(BF16)
    feature = lax.broadcasted_iota(jnp.int32, (q_t.shape[0], 1), 0)
    for head in range(2):
        qh_scr[2 * buf + head] = jnp.where((feature % PAIR) // HEAD_DIM == head, q_t, jnp.zeros_like(q_t))


def _produce_scores(k_ref, qh_scr, s_scr, head, j, slot, buf):
    start = pl.multiple_of(j * ATTN_TK, ATTN_TK)
    s_scr[slot] = _dot(k_ref[pl.ds(start, ATTN_TK), :], qh_scr[2 * buf + head])


def _attend_query_tile(c_first_ref, c_last_ref, floor_ref, q_ref, k_ref, vt_ref, gate_ref, o_ref,
                       qh_scr, s_scr, m_scr, acc_scr, plan_scr, n_tiles, b, pair, qi):
    decay_floor = floor_ref[0]
    q_rows = pl.ds(pl.multiple_of(qi * ATTN_TQ, ATTN_TQ), ATTN_TQ)
    next_qi = jnp.minimum(qi + 1, n_tiles - 1)
    buf = qi % 2

    def first_needed_tiles(tile):
        def one_head(head):
            base = (b * N_HEADS + 2 * pair + head) * n_tiles
            c_query = c_first_ref[base + tile]
            return lax.fori_loop(
                0, tile, lambda j, n: n + jnp.where(c_query - c_last_ref[base + j] < decay_floor, 1, 0), 0)

        f0, f1 = one_head(0), one_head(1)
        odd = (f0 + f1) % 2
        return f0 - jnp.where((odd == 1) & (f0 > 0), 1, 0), f1 - jnp.where((odd == 1) & (f0 == 0), 1, 0)

    def item_of(u, tile, f0, f1):
        n0 = tile - f0
        in0, in1 = u < n0, u < n0 + tile - f1
        return jnp.where(in0, 0, jnp.where(in1, 1, 0)), jnp.where(in0, f0 + u, jnp.where(in1, f1 + u - n0, tile))

    first0, first1 = plan_scr[0], plan_scr[1]
    next_firsts = first_needed_tiles(next_qi)
    plan_scr[0], plan_scr[1] = next_firsts
    n_full = 2 * qi - first0 - first1

    def item(u):
        return item_of(u, qi, first0, first1)

    ones = jnp.ones((HEAD_DIM, ATTN_TK), BF16)

    def produce(head, j, slot):
        _produce_scores(k_ref, qh_scr, s_scr, head, j, slot, buf)

    def consume(head, j, slot, on_diagonal):
        s = s_scr[slot]
        if on_diagonal:
            key_pos = lax.broadcasted_iota(jnp.int32, s.shape, 0)
            query_pos = lax.broadcasted_iota(jnp.int32, s.shape, 1)
            s = jnp.where(key_pos <= query_pos, s, -jnp.inf)
        m = m_scr[head]
        m_new = jnp.maximum(m, jnp.max(s, axis=0, keepdims=True))
        p = jnp.exp2(s - m_new).astype(BF16)
        v_rows = pl.ds(pl.multiple_of(head * HEAD_DIM, HEAD_DIM), HEAD_DIM)
        v_aug = jnp.concatenate([vt_ref[j, v_rows, :], ones], axis=0)
        acc_scr[head] = jnp.exp2(m - m_new) * acc_scr[head] + _dot(v_aug, p)
        m_scr[head] = m_new

    def pipelined(u, n_items):
        for k in range(n_items):
            produce(*item(u + k + 1), (k + 1) % n_items)
            consume(*item(u + k), k, False)

    m_scr[...] = jnp.full(m_scr.shape, -jnp.inf, F32)
    acc_scr[...] = jnp.zeros(acc_scr.shape, F32)

    @pl.loop(0, n_full // SCORE_SLOTS)
    def _(i):
        pipelined(SCORE_SLOTS * i, SCORE_SLOTS)

    done = n_full - n_full % SCORE_SLOTS
    run = SCORE_SLOTS // 2
    while run >= 2:
        @pl.when(n_full & run != 0)
        def _(done=done, run=run):
            pipelined(done, run)
        done = done + (n_full & run)
        run //= 2

    produce(1, qi, 1)
    consume(0, qi, 0, True)
    _stage_queries(q_ref, qh_scr, next_qi, 1 - buf)
    _produce_scores(k_ref, qh_scr, s_scr, *item_of(0, next_qi, *next_firsts), 0, 1 - buf)
    consume(1, qi, 1, True)

    o_t = jnp.concatenate([acc_scr[head, :HEAD_DIM, :] / acc_scr[head, HEAD_DIM:, :] for head in range(2)], axis=0)
    o_ref[q_rows, :] = (o_t.T * gate_ref[q_rows, :]).astype(o_ref.dtype)


def _resident(shape):
    return pl.BlockSpec(shape, lambda *_: (0,) * len(shape), pipeline_mode=pl.Buffered(1))


def _row_tile(cols):
    return pl.BlockSpec((None, ROW_TILE, cols), lambda b, s: (b, s, 0))


def _dense_call(kernel, name, grid, in_specs, out_specs, out_shape, scratch_shapes=()):
    return pl.pallas_call(
        kernel, name=name, grid=grid, in_specs=in_specs, out_specs=out_specs, out_shape=out_shape,
        scratch_shapes=list(scratch_shapes),
        compiler_params=pltpu.CompilerParams(
            dimension_semantics=("arbitrary", "arbitrary"), vmem_limit_bytes=VMEM_LIMIT))


def _placement_constants():
    d = N_HEADS * HEAD_DIM
    head_of_col = jnp.arange(d) // HEAD_DIM
    pos_in_head = jnp.arange(d) % HEAD_DIM
    lane = jnp.arange(LANES)
    red = (head_of_col[:, None] == lane[None, :]).astype(F32) / HEAD_DIM
    red = jnp.concatenate([red, red], axis=0).astype(BF16)
    bcast = (lane[:, None] == head_of_col[None, :]).astype(BF16)
    bcast = jnp.concatenate([bcast, bcast], axis=0)
    piece, head = lane // N_HEADS, lane % N_HEADS
    valid = (piece < 3)[:, None] & (head[:, None] == head_of_col[None, :])
    place_q = (valid & (pos_in_head[None, :] == piece[:, None])).astype(BF16)
    place_k = -(valid & (pos_in_head[None, :] == 3 + piece[:, None])).astype(BF16)
    const_q = ((pos_in_head >= 3) & (pos_in_head < 6)).astype(F32)[None, :]
    const_k = (pos_in_head < 3).astype(F32)[None, :]
    return red, bcast, place_q, const_q, place_k, const_k


def kernel(x, attn_norm, ffn_norm, a_w_in, a_conv, a_w_out, kv_norm, w_kvf, b_f, k_norm,
           b_w_qg, q_norm, b_w_out, ffn_w_up, ffn_conv, ffn_w_down):
    bsz, s_len, d = x.shape
    depth = ffn_w_up.shape[0]
    n_a = a_w_in.shape[0]
    f = ffn_w_down.shape[1]
    assert d == N_HEADS * HEAD_DIM and s_len % ROW_TILE == 0 and s_len % ATTN_TQ == 0
    assert ATTN_TQ == ATTN_TK == ROW_TILE and d % CHUNK == 0 and f % CHUNK == 0

    grid = (bsz, s_len // ROW_TILE)
    x_shape = jax.ShapeDtypeStruct((bsz, s_len, d), F32)
    red, bcast, place_q, const_q, place_k, const_k = _placement_constants()
    row = lambda v: v.reshape(1, -1)

    def ffn_operands(l):
        return (row(ffn_norm[l]), ffn_w_up[l].astype(BF16), ffn_conv[l], ffn_w_down[l].astype(BF16))

    ffn_specs = [_resident((1, d)), _resident((d, 2 * f)), _resident((3, f)), _resident((f, d))]
    ffn_carry = pltpu.VMEM((CARRY_ROWS, f), F32)

    layer_a = _dense_call(
        _layer_a_kernel, "layer_a", grid,
        [_row_tile(d), _resident((1, d)), _resident((d, 3 * d)), _resident((3, d)), _resident((d, d))] + ffn_specs,
        _row_tile(d), x_shape, [pltpu.VMEM((CARRY_ROWS, d), F32), ffn_carry])
    for l in range(n_a):
        x = layer_a(x, row(attn_norm[l]), a_w_in[l].astype(BF16), a_conv[l], a_w_out[l].astype(BF16),
                    *ffn_operands(l))

    w_f = w_kvf[:, 2 * d:]
    pad = LANES - 3 * N_HEADS
    w_f3 = jnp.pad(jnp.concatenate([w_f, w_f, w_f], axis=1), ((0, 0), (0, pad))).astype(BF16)
    b_f3 = jnp.pad(jnp.concatenate([b_f, b_f, b_f]), (0, pad)).reshape(1, LANES)
    tri = (jnp.arange(ROW_TILE)[:, None] >= jnp.arange(ROW_TILE)[None, :]).astype(BF16)
    n_kv_tiles = s_len // ATTN_TK
    kcat, vt, cp, c_edges = _dense_call(
        _kv_kernel, "kv_proj", grid,
        [_row_tile(d), _resident((1, d)), _resident((d, d)), _resident((d, d)), _resident((d, LANES)),
         _resident((1, LANES)), _resident((1, d)), _resident((2 * d, LANES)), _resident((2 * LANES, d)),
         _resident((ROW_TILE, ROW_TILE)), _resident((LANES, d)), _resident((1, d))],
        [_row_tile(2 * d), pl.BlockSpec((None, None, d, ATTN_TK), lambda b, s: (b, s, 0, 0)), _row_tile(LANES),
         pl.BlockSpec((None, None, CARRY_ROWS, LANES), lambda b, s: (b, s, 0, 0))],
        [jax.ShapeDtypeStruct((bsz, s_len, 2 * d), BF16),
         jax.ShapeDtypeStruct((bsz, n_kv_tiles, d, ATTN_TK), BF16),
         jax.ShapeDtypeStruct((bsz, s_len, LANES), BF16),
         jax.ShapeDtypeStruct((bsz, n_kv_tiles, CARRY_ROWS, LANES), F32)],
        [pltpu.VMEM((CARRY_ROWS, LANES), F32)],
    )(x, row(kv_norm), w_kvf[:, :d].astype(BF16), w_kvf[:, d:2 * d].astype(BF16), w_f3, b_f3,
      row(jnp.tile(k_norm, N_HEADS)), red, bcast, tri, place_k, const_k)
    c_first = c_edges[:, :, 0, :N_HEADS].transpose(0, 2, 1).reshape(-1)
    c_last = c_edges[:, :, 1, :N_HEADS].transpose(0, 2, 1).reshape(-1)

    q_proj = _dense_call(
        _q_kernel, "q_proj", grid,
        [_row_tile(d), _resident((1, d)), _resident((d, d)), _resident((d, d)), _resident((1, d)),
         _row_tile(LANES), _resident((2 * d, LANES)), _resident((2 * LANES, d)),
         _resident((LANES, d)), _resident((1, d))],
        [_row_tile(2 * d), _row_tile(d)],
        [jax.ShapeDtypeStruct((bsz, s_len, 2 * d), BF16), x_shape])

    n_pairs = d // PAIR
    attention = pl.pallas_call(
        _attn_kernel, name="fox_attention",
        grid_spec=pltpu.PrefetchScalarGridSpec(
            num_scalar_prefetch=3, grid=(bsz, n_pairs),
            in_specs=[pl.BlockSpec((None, s_len, 2 * PAIR), lambda b, p, *_: (b, 0, p)),
                      pl.BlockSpec((None, s_len, 2 * PAIR), lambda b, p, *_: (b, 0, p)),
                      pl.BlockSpec((None, n_kv_tiles, PAIR, ATTN_TK), lambda b, p, *_: (b, 0, p, 0)),
                      pl.BlockSpec((None, s_len, PAIR), lambda b, p, *_: (b, 0, p))],
            out_specs=pl.BlockSpec((None, s_len, PAIR), lambda b, p, *_: (b, 0, p)),
            scratch_shapes=[pltpu.VMEM((4, 2 * PAIR, ATTN_TQ), BF16),
                            pltpu.VMEM((SCORE_SLOTS, ATTN_TK, ATTN_TQ), F32),
                            pltpu.VMEM((2, 1, ATTN_TQ), F32), pltpu.VMEM((2, PAIR, ATTN_TQ), F32),
                            pltpu.SMEM((2,), jnp.int32)]),
        out_shape=jax.ShapeDtypeStruct((bsz, s_len, d), BF16),
        compiler_params=pltpu.CompilerParams(
            dimension_semantics=("arbitrary",) * 2, vmem_limit_bytes=VMEM_LIMIT))
    k_gain = jnp.max(jnp.abs(k_norm))

    def decay_floor(q_gain):
        qk_bound = 1.02 * HEAD_DIM ** 0.5 * jnp.max(jnp.abs(q_gain)) * k_gain
        return (-(2.0 * qk_bound + NEGLIGIBLE_LOG_WEIGHT)).reshape(1).astype(F32)

    layer_b = _dense_call(
        _layer_b_kernel, "layer_b", grid,
        [_row_tile(d), _row_tile(d), _resident((d, d))] + ffn_specs,
        _row_tile(d), x_shape, [ffn_carry])

    scale = HEAD_DIM ** -0.5 * LOG2_E
    for l in range(n_a, depth):
        j = l - n_a
        qcat, gate = q_proj(x, row(attn_norm[l]), b_w_qg[j][:, :d].astype(BF16), b_w_qg[j][:, d:].astype(BF16),
                            row(jnp.tile(q_norm[j], N_HEADS)) * scale, cp, red, bcast, place_q, const_q)
        og = attention(c_first, c_last, decay_floor(q_norm[j]), qcat, kcat, vt, gate)
        x = layer_b(x, og, b_w_out[j].astype(BF16), *ffn_operands(l))
    return x
```

```python
import functools

import jax
import jax.numpy as jnp
from jax import lax
from jax.experimental import pallas as pl
from jax.experimental.pallas import tpu as pltpu

F32 = jnp.float32
BF16 = jnp.bfloat16

EPS = 1e-6
LOG2_E = 1.4426950408889634
N_HEADS = 16
HEAD_DIM = 64
LANES = 128
PAIR = 2 * HEAD_DIM
CARRY_ROWS = 8

ROW_TILE = 512
CHUNK = 256
ATTN_TQ = 512
ATTN_TK = 512
SCORE_SLOTS = 16
NEGLIGIBLE_LOG_WEIGHT = 88.0
VMEM_LIMIT = 56 * 1024 * 1024


def _dot(a, b):
    return jnp.dot(a, b, preferred_element_type=F32)


def _rms(x, g):
    return x * lax.rsqrt(jnp.mean(x * x, axis=-1, keepdims=True) + EPS) * g


def _split2(x):
    hi = x.astype(BF16)
    lo = (x - hi.astype(F32)).astype(BF16)
    return hi, lo


def _split3(x):
    p1 = x.astype(BF16)
    r1 = x - p1.astype(F32)
    p2 = r1.astype(BF16)
    p3 = (r1 - p2.astype(F32)).astype(BF16)
    return p1, p2, p3


def _head_rms(q, red_ref, bcast_ref):
    sq_hi, sq_lo = _split2(q * q)
    ms = _dot(jnp.concatenate([sq_hi, sq_lo], axis=1), red_ref[...])
    r_hi, r_lo = _split2(lax.rsqrt(ms + EPS))
    return q * _dot(jnp.concatenate([r_hi, r_lo], axis=1), bcast_ref[...])


def _shift_down(a, first_row):
    row = lax.broadcasted_iota(jnp.int32, a.shape, 0)
    return jnp.where(row == 0, first_row, pltpu.roll(a, 1, axis=0))


def _causal_conv3(u, w, carry_ref, lo, hi):
    rows = u.shape[0]
    old = carry_ref[:, lo:hi]
    carry_ref[:, lo:hi] = u[rows - CARRY_ROWS:, :]
    u1 = _shift_down(u, old[CARRY_ROWS - 1:CARRY_ROWS, :])
    u2 = _shift_down(u1, old[CARRY_ROWS - 2:CARRY_ROWS - 1, :])
    return w[2:3, :] * u + w[1:2, :] * u1 + w[0:1, :] * u2


def _mixer(x, gn_ref, w_in_ref, cw_ref, w_out_ref, carry_ref):
    d = x.shape[1]
    xn = _rms(x, gn_ref[...]).astype(BF16)

    def project(j):
        return tuple(_dot(xn, w_in_ref[:, part * d + j * CHUNK:part * d + (j + 1) * CHUNK]) for part in range(3))

    n_chunks = d // CHUNK
    acc = jnp.zeros(x.shape, F32)
    ahead = project(0)
    for j in range(n_chunks):
        lo, hi = j * CHUNK, (j + 1) * CHUNK
        b, c, h = ahead
        if j + 1 < n_chunks:
            ahead = project(j + 1)
        u = _causal_conv3(c * h, cw_ref[:, lo:hi], carry_ref, lo, hi)
        acc = acc + _dot((b * u).astype(BF16), w_out_ref[lo:hi, :])
    return x + acc


def _conv_ffn(x, gn_ref, w_up_ref, cw_ref, w_down_ref, carry_ref):
    f = w_down_ref.shape[0]
    xn = _rms(x, gn_ref[...]).astype(BF16)

    def project(j):
        return tuple(_dot(xn, w_up_ref[:, part * f + j * CHUNK:part * f + (j + 1) * CHUNK]) for part in range(2))

    n_chunks = f // CHUNK
    acc = jnp.zeros(x.shape, F32)
    ahead = project(0)
    for j in range(n_chunks):
        lo, hi = j * CHUNK, (j + 1) * CHUNK
        a, g = ahead
        if j + 1 < n_chunks:
            ahead = project(j + 1)
        a = _causal_conv3(a, cw_ref[:, lo:hi], carry_ref, lo, hi)
        hid = a * jax.nn.sigmoid(a) * g
        acc = acc + _dot(hid.astype(BF16), w_down_ref[lo:hi, :])
    return x + acc


def _row_halves(rows):
    return [pl.ds(0, rows // 2), pl.ds(rows // 2, rows // 2)]


def _reset_at_sequence_start(*carry_refs):
    @pl.when(pl.program_id(1) == 0)
    def _():
        for ref in carry_refs:
            ref[...] = jnp.zeros(ref.shape, ref.dtype)


def _layer_a_kernel(x_ref, an_ref, w_in_ref, acw_ref, w_out_ref,
                    fn_ref, w_up_ref, fcw_ref, w_down_ref, o_ref,
                    mix_carry, ffn_carry):
    _reset_at_sequence_start(mix_carry, ffn_carry)
    halves = _row_halves(x_ref.shape[0])
    mixed = [_mixer(x_ref[rows, :], an_ref, w_in_ref, acw_ref, w_out_ref, mix_carry) for rows in halves]
    for rows, x in zip(halves, mixed):
        o_ref[rows, :] = _conv_ffn(x, fn_ref, w_up_ref, fcw_ref, w_down_ref, ffn_carry)


def _layer_b_kernel(x_ref, og_ref, w_out_ref,
                    fn_ref, w_up_ref, fcw_ref, w_down_ref, o_ref, ffn_carry):
    _reset_at_sequence_start(ffn_carry)
    halves = _row_halves(x_ref.shape[0])
    attended = [x_ref[rows, :] + _dot(og_ref[rows, :], w_out_ref[...]) for rows in halves]
    for rows, x in zip(halves, attended):
        o_ref[rows, :] = _conv_ffn(x, fn_ref, w_up_ref, fcw_ref, w_down_ref, ffn_carry)


def _interleave_pairs(main, extra, out_ref, rows):
    for p in range(main.shape[1] // PAIR):
        out_ref[rows, 2 * p * PAIR:(2 * p + 1) * PAIR] = main[:, p * PAIR:(p + 1) * PAIR].astype(out_ref.dtype)
        out_ref[rows, (2 * p + 1) * PAIR:(2 * p + 2) * PAIR] = extra[:, p * PAIR:(p + 1) * PAIR].astype(out_ref.dtype)


def _kv_kernel(x_ref, gn_ref, wk_ref, wv_ref, wf_ref, bf_ref, kg_ref,
               red_ref, bcast_ref, tri_ref, place_ref, const_ref,
               kcat_ref, vt_ref, cp_ref, edge_ref, c_carry):
    _reset_at_sequence_start(c_carry)
    halves = _row_halves(x_ref.shape[0])
    half = x_ref.shape[0] // 2
    tri = tri_ref[:half, :half]
    edges = []
    for rows in halves:
        h = _rms(x_ref[rows, :], gn_ref[...]).astype(BF16)
        vt_ref[:, rows] = _dot(h, wv_ref[...]).T.astype(vt_ref.dtype)

        f_logit = _dot(h, wf_ref[...]) + bf_ref[...]
        log_f = jnp.minimum(f_logit, 0.0) - jnp.log1p(jnp.exp(-jnp.abs(f_logit)))
        c = c_carry[0:1, :] + sum(_dot(tri, piece) for piece in _split3(log_f))
        last = c[half - 1:, :]
        c_carry[0:1, :] = last
        edges += [c[0:1, :], last]
        c1, c2, c3 = _split3(c * LOG2_E)
        lane = lax.broadcasted_iota(jnp.int32, c.shape, 1)
        cp = jnp.where(lane < N_HEADS, c1, jnp.where(lane < 2 * N_HEADS, c2, c3))
        cp_ref[rows, :] = cp

        k = _head_rms(_dot(h, wk_ref[...]), red_ref, bcast_ref) * kg_ref[...]
        extra = _dot(cp, place_ref[...]) + const_ref[...]
        _interleave_pairs(k, extra, kcat_ref, rows)
    edge_row = lax.broadcasted_iota(jnp.int32, edge_ref.shape, 0)
    edge_ref[...] = jnp.where(edge_row == 0, edges[0], jnp.where(edge_row == 1, edges[-1], 0.0))


def _q_kernel(x_ref, gn_ref, wq_ref, wg_ref, qg_ref, cp_ref,
              red_ref, bcast_ref, place_ref, const_ref,
              qcat_ref, gate_ref):
    for rows in _row_halves(x_ref.shape[0]):
        xn = _rms(x_ref[rows, :], gn_ref[...]).astype(BF16)
        gate_ref[rows, :] = jax.nn.sigmoid(_dot(xn, wg_ref[...]))
        q = _head_rms(_dot(xn, wq_ref[...]), red_ref, bcast_ref) * qg_ref[...]
        extra = _dot(cp_ref[rows, :], place_ref[...]) + const_ref[...]
        _interleave_pairs(q, extra, qcat_ref, rows)


def _attn_kernel(c_first_ref, c_last_ref, floor_ref, q_ref, k_ref, vt_ref, gate_ref, o_ref,
                 qh_scr, s_scr, m_scr, acc_scr, plan_scr):
    n_tiles = q_ref.shape[0] // ATTN_TQ
    plan_scr[0] = 0
    plan_scr[1] = 0
    _stage_queries(q_ref, qh_scr, 0, 0)
    _produce_scores(k_ref, qh_scr, s_scr, 0, 0, 0, 0)
    pl.loop(0, n_tiles)(functools.partial(
        _attend_query_tile, c_first_ref, c_last_ref, floor_ref, q_ref, k_ref, vt_ref, gate_ref, o_ref,
        qh_scr, s_scr, m_scr, acc_scr, plan_scr, n_tiles, pl.program_id(0), pl.program_id(1)))


def _stage_queries(q_ref, qh_scr, tile, buf):
    rows = pl.ds(pl.multiple_of(tile * ATTN_TQ, ATTN_TQ), ATTN_TQ)
    q_t = q_ref[rows, :].astype(F32).T.astype(BF16)
    feature = lax.broadcasted_iota(jnp.int32, (q_t.shape[0], 1), 0)
    for head in range(2):
        qh_scr[2 * buf + head] = jnp.where((feature % PAIR) // HEAD_DIM == head, q_t, jnp.zeros_like(q_t))


def _produce_scores(k_ref, qh_scr, s_scr, head, j, slot, buf):
    start = pl.multiple_of(j * ATTN_TK, ATTN_TK)
    s_scr[slot] = _dot(k_ref[pl.ds(start, ATTN_TK), :], qh_scr[2 * buf + head])


def _attend_query_tile(c_first_ref, c_last_ref, floor_ref, q_ref, k_ref, vt_ref, gate_ref, o_ref,
                       qh_scr, s_scr, m_scr, acc_scr, plan_scr, n_tiles, b, pair, qi):
    decay_floor = floor_ref[0]
    q_rows = pl.ds(pl.multiple_of(qi * ATTN_TQ, ATTN_TQ), ATTN_TQ)
    next_qi = jnp.minimum(qi + 1, n_tiles - 1)
    buf = qi % 2

    def first_needed_tiles(tile):
        def one_head(head):
            base = (b * N_HEADS + 2 * pair + head) * n_tiles
            c_query = c_first_ref[base + tile]
            return lax.fori_loop(
                0, tile, lambda j, n: n + jnp.where(c_query - c_last_ref[base + j] < decay_floor, 1, 0), 0)

        f0, f1 = one_head(0), one_head(1)
        odd = (f0 + f1) % 2
        return f0 - jnp.where((odd == 1) & (f0 > 0), 1, 0), f1 - jnp.where((odd == 1) & (f0 == 0), 1, 0)

    def item_of(u, tile, f0, f1):
        n0 = tile - f0
        in0, in1 = u < n0, u < n0 + tile - f1
        return jnp.where(in0, 0, jnp.where(in1, 1, 0)), jnp.where(in0, f0 + u, jnp.where(in1, f1 + u - n0, tile))

    first0, first1 = plan_scr[0], plan_scr[1]
    next_firsts = first_needed_tiles(next_qi)
    plan_scr[0], plan_scr[1] = next_firsts
    n_full = 2 * qi - first0 - first1

    def item(u):
        return item_of(u, qi, first0, first1)

    ones = jnp.ones((HEAD_DIM, ATTN_TK), BF16)

    def produce(head, j, slot):
        _produce_scores(k_ref, qh_scr, s_scr, head, j, slot, buf)

    def consume(head, j, slot, on_diagonal):
        s = s_scr[slot]
        if on_diagonal:
            key_pos = lax.broadcasted_iota(jnp.int32, s.shape, 0)
            query_pos = lax.broadcasted_iota(jnp.int32, s.shape, 1)
            s = jnp.where(key_pos <= query_pos, s, -jnp.inf)
        m = m_scr[head]
        m_new = jnp.maximum(m, jnp.max(s, axis=0, keepdims=True))
        p = jnp.exp2(s - m_new).astype(BF16)
        v_rows = pl.ds(pl.multiple_of(head * HEAD_DIM, HEAD_DIM), HEAD_DIM)
        v_aug = jnp.concatenate([vt_ref[j, v_rows, :], ones], axis=0)
        acc_scr[head] = jnp.exp2(m - m_new) * acc_scr[head] + _dot(v_aug, p)
        m_scr[head] = m_new

    def pipelined(u, n_items):
        for k in range(n_items):
            produce(*item(u + k + 1), (k + 1) % n_items)
            consume(*item(u + k), k, False)

    m_scr[...] = jnp.full(m_scr.shape, -jnp.inf, F32)
    acc_scr[...] = jnp.zeros(acc_scr.shape, F32)

    @pl.loop(0, n_full // SCORE_SLOTS)
    def _(i):
        pipelined(SCORE_SLOTS * i, SCORE_SLOTS)

    done = n_full - n_full % SCORE_SLOTS
    run = SCORE_SLOTS // 2
    while run >= 2:
        @pl.when(n_full & run != 0)
        def _(done=done, run=run):
            pipelined(done, run)
        done = done + (n_full & run)
        run //= 2

    produce(1, qi, 1)
    consume(0, qi, 0, True)
    _stage_queries(q_ref, qh_scr, next_qi, 1 - buf)
    _produce_scores(k_ref, qh_scr, s_scr, *item_of(0, next_qi, *next_firsts), 0, 1 - buf)
    consume(1, qi, 1, True)

    o_t = jnp.concatenate([acc_scr[head, :HEAD_DIM, :] / acc_scr[head, HEAD_DIM:, :] for head in range(2)], axis=0)
    o_ref[q_rows, :] = (o_t.T * gate_ref[q_rows, :]).astype(o_ref.dtype)


def _resident(shape):
    return pl.BlockSpec(shape, lambda *_: (0,) * len(shape), pipeline_mode=pl.Buffered(1))


def _row_tile(cols):
    return pl.BlockSpec((None, ROW_TILE, cols), lambda b, s: (b, s, 0))


def _dense_call(kernel, name, grid, in_specs, out_specs, out_shape, scratch_shapes=()):
    return pl.pallas_call(
        kernel, name=name, grid=grid, in_specs=in_specs, out_specs=out_specs, out_shape=out_shape,
        scratch_shapes=list(scratch_shapes),
        compiler_params=pltpu.CompilerParams(
            dimension_semantics=("arbitrary", "arbitrary"), vmem_limit_bytes=VMEM_LIMIT))


def _placement_constants():
    d = N_HEADS * HEAD_DIM
    head_of_col = jnp.arange(d) // HEAD_DIM
    pos_in_head = jnp.arange(d) % HEAD_DIM
    lane = jnp.arange(LANES)
    red = (head_of_col[:, None] == lane[None, :]).astype(F32) / HEAD_DIM
    red = jnp.concatenate([red, red], axis=0).astype(BF16)
    bcast = (lane[:, None] == head_of_col[None, :]).astype(BF16)
    bcast = jnp.concatenate([bcast, bcast], axis=0)
    piece, head = lane // N_HEADS, lane % N_HEADS
    valid = (piece < 3)[:, None] & (head[:, None] == head_of_col[None, :])
    place_q = (valid & (pos_in_head[None, :] == piece[:, None])).astype(BF16)
    place_k = -(valid & (pos_in_head[None, :] == 3 + piece[:, None])).astype(BF16)
    const_q = ((pos_in_head >= 3) & (pos_in_head < 6)).astype(F32)[None, :]
    const_k = (pos_in_head < 3).astype(F32)[None, :]
    return red, bcast, place_q, const_q, place_k, const_k


def kernel(x, attn_norm, ffn_norm, a_w_in, a_conv, a_w_out, kv_norm, w_kvf, b_f, k_norm,
           b_w_qg, q_norm, b_w_out, ffn_w_up, ffn_conv, ffn_w_down):
    bsz, s_len, d = x.shape
    depth = ffn_w_up.shape[0]
    n_a = a_w_in.shape[0]
    f = ffn_w_down.shape[1]
    assert d == N_HEADS * HEAD_DIM and s_len % ROW_TILE == 0 and s_len % ATTN_TQ == 0
    assert ATTN_TQ == ATTN_TK == ROW_TILE and d % CHUNK == 0 and f % CHUNK == 0

    grid = (bsz, s_len // ROW_TILE)
    x_shape = jax.ShapeDtypeStruct((bsz, s_len, d), F32)
    red, bcast, place_q, const_q, place_k, const_k = _placement_constants()
    row = lambda v: v.reshape(1, -1)

    def ffn_operands(l):
        return (row(ffn_norm[l]), ffn_w_up[l].astype(BF16), ffn_conv[l], ffn_w_down[l].astype(BF16))

    ffn_specs = [_resident((1, d)), _resident((d, 2 * f)), _resident((3, f)), _resident((f, d))]
    ffn_carry = pltpu.VMEM((CARRY_ROWS, f), F32)

    layer_a = _dense_call(
        _layer_a_kernel, "layer_a", grid,
        [_row_tile(d), _resident((1, d)), _resident((d, 3 * d)), _resident((3, d)), _resident((d, d))] + ffn_specs,
        _row_tile(d), x_shape, [pltpu.VMEM((CARRY_ROWS, d), F32), ffn_carry])
    for l in range(n_a):
        x = layer_a(x, row(attn_norm[l]), a_w_in[l].astype(BF16), a_conv[l], a_w_out[l].astype(BF16),
                    *ffn_operands(l))

    w_f = w_kvf[:, 2 * d:]
    pad = LANES - 3 * N_HEADS
    w_f3 = jnp.pad(jnp.concatenate([w_f, w_f, w_f], axis=1), ((0, 0), (0, pad))).astype(BF16)
    b_f3 = jnp.pad(jnp.concatenate([b_f, b_f, b_f]), (0, pad)).reshape(1, LANES)
    tri = (jnp.arange(ROW_TILE)[:, None] >= jnp.arange(ROW_TILE)[None, :]).astype(BF16)
    n_kv_tiles = s_len // ATTN_TK
    kcat, vt, cp, c_edges = _dense_call(
        _kv_kernel, "kv_proj", grid,
        [_row_tile(d), _resident((1, d)), _resident((d, d)), _resident((d, d)), _resident((d, LANES)),
         _resident((1, LANES)), _resident((1, d)), _resident((2 * d, LANES)), _resident((2 * LANES, d)),
         _resident((ROW_TILE, ROW_TILE)), _resident((LANES, d)), _resident((1, d))],
        [_row_tile(2 * d), pl.BlockSpec((None, None, d, ATTN_TK), lambda b, s: (b, s, 0, 0)), _row_tile(LANES),
         pl.BlockSpec((None, None, CARRY_ROWS, LANES), lambda b, s: (b, s, 0, 0))],
        [jax.ShapeDtypeStruct((bsz, s_len, 2 * d), BF16),
         jax.ShapeDtypeStruct((bsz, n_kv_tiles, d, ATTN_TK), BF16),
         jax.ShapeDtypeStruct((bsz, s_len, LANES), BF16),
         jax.ShapeDtypeStruct((bsz, n_kv_tiles, CARRY_ROWS, LANES), F32)],
        [pltpu.VMEM((CARRY_ROWS, LANES), F32)],
    )(x, row(kv_norm), w_kvf[:, :d].astype(BF16), w_kvf[:, d:2 * d].astype(BF16), w_f3, b_f3,
      row(jnp.tile(k_norm, N_HEADS)), red, bcast, tri, place_k, const_k)
    c_first = c_edges[:, :, 0, :N_HEADS].transpose(0, 2, 1).reshape(-1)
    c_last = c_edges[:, :, 1, :N_HEADS].transpose(0, 2, 1).reshape(-1)

    q_proj = _dense_call(
        _q_kernel, "q_proj", grid,
        [_row_tile(d), _resident((1, d)), _resident((d, d)), _resident((d, d)), _resident((1, d)),
         _row_tile(LANES), _resident((2 * d, LANES)), _resident((2 * LANES, d)),
         _resident((LANES, d)), _resident((1, d))],
        [_row_tile(2 * d), _row_tile(d)],
        [jax.ShapeDtypeStruct((bsz, s_len, 2 * d), BF16), x_shape])

    n_pairs = d // PAIR
    attention = pl.pallas_call(
        _attn_kernel, name="fox_attention",
        grid_spec=pltpu.PrefetchScalarGridSpec(
            num_scalar_prefetch=3, grid=(bsz, n_pairs),
            in_specs=[pl.BlockSpec((None, s_len, 2 * PAIR), lambda b, p, *_: (b, 0, p)),
                      pl.BlockSpec((None, s_len, 2 * PAIR), lambda b, p, *_: (b, 0, p)),
                      pl.BlockSpec((None, n_kv_tiles, PAIR, ATTN_TK), lambda b, p, *_: (b, 0, p, 0)),
                      pl.BlockSpec((None, s_len, PAIR), lambda b, p, *_: (b, 0, p))],
            out_specs=pl.BlockSpec((None, s_len, PAIR), lambda b, p, *_: (b, 0, p)),
            scratch_shapes=[pltpu.VMEM((4, 2 * PAIR, ATTN_TQ), BF16),
                            pltpu.VMEM((SCORE_SLOTS, ATTN_TK, ATTN_TQ), F32),
                            pltpu.VMEM((2, 1, ATTN_TQ), F32), pltpu.VMEM((2, PAIR, ATTN_TQ), F32),
                            pltpu.SMEM((2,), jnp.int32)]),
        out_shape=jax.ShapeDtypeStruct((bsz, s_len, d), BF16),
        compiler_params=pltpu.CompilerParams(
            dimension_semantics=("arbitrary",) * 2, vmem_limit_bytes=VMEM_LIMIT))
    k_gain = jnp.max(jnp.abs(k_norm))

    def decay_floor(q_gain):
        qk_bound = 1.02 * HEAD_DIM ** 0.5 * jnp.max(jnp.abs(q_gain)) * k_gain
        return (-(2.0 * qk_bound + NEGLIGIBLE_LOG_WEIGHT)).reshape(1).astype(F32)

    layer_b = _dense_call(
        _layer_b_kernel, "layer_b", grid,
        [_row_tile(d), _row_tile(d), _resident((d, d))] + ffn_specs,
        _row_tile(d), x_shape, [ffn_carry])

    scale = HEAD_DIM ** -0.5 * LOG2_E
    for l in range(n_a, depth):
        j = l - n_a
        qcat, gate = q_proj(x, row(attn_norm[l]), b_w_qg[j][:, :d].astype(BF16), b_w_qg[j][:, d:].astype(BF16),
                            row(jnp.tile(q_norm[j], N_HEADS)) * scale, cp, red, bcast, place_q, const_q)
        og = attention(c_first, c_last, decay_floor(q_norm[j]), qcat, kcat, vt, gate)
        x = layer_b(x, og, b_w_out[j].astype(BF16), *ffn_operands(l))
    return x
```
